```python
import math
import jax, jax.numpy as jnp
from jax import lax
import numpy as np

D_MODEL = 1024
BATCH = 4
SEQ = 4096
DEPTH = 4
DEC_BATCH = 128
DEC_SEQ = 8
PAST_LEN = 2048
PAGE_SIZE = 128

N_A_LAYERS = DEPTH // 2
N_B_LAYERS = DEPTH - N_A_LAYERS
HEAD_DIM = 64
A_PATTERNS = ((128, 1), (512, 4), (2048, 16))
N_GROUPS_A = len(A_PATTERNS)
N_HEADS_A = D_MODEL // HEAD_DIM
Q_BLK = 128
N_HEADS_B = D_MODEL // HEAD_DIM
N_KV_B = 4
HPG_B = N_HEADS_B // N_KV_B
CMP_LEN = 32
CMP_STRIDE = 16
CMP_HIDDEN = 2 * HEAD_DIM
SLC_LEN = 64
N_SELECT = 16
WIN_B = 512
N_KV_ROWS = 4
NSA_Q_BLK = 64
FORCE_SCORE = 1.0e4
D_FF = -(-8 * D_MODEL // (3 * 256)) * 256
ROPE_THETA = 10000.0
EPS = 1e-6
NEG = -1e30

kernel_name = 'yoco_dilated_nsa_decoder_step'


def rms_norm(x, g):
    xf = x.astype(jnp.float32)
    y = xf * lax.rsqrt(jnp.mean(xf * xf, axis=-1, keepdims=True) + EPS)
    return (y * g.astype(jnp.float32)).astype(x.dtype)


def modulate(h, shift, scale):
    return h * (1.0 + scale) + shift


def adaln(c, w, b, n):
    return (jax.nn.silu(c) @ w + b).reshape(c.shape[0], n, 1, D_MODEL)


def swiglu(h, w_in, w_out):
    a, g = jnp.split(h @ w_in, 2, axis=-1)
    return (jax.nn.silu(a) * g) @ w_out


def rope(x, pos):
    half = x.shape[-1] // 2
    inv_freq = ROPE_THETA ** (-jnp.arange(half, dtype=jnp.float32) / half)
    ang = pos.astype(jnp.float32)[:, None] * inv_freq[None, :]
    bshape = (1, pos.shape[0]) + (1,) * (x.ndim - 3) + (half,)
    cos = jnp.cos(ang).reshape(bshape)
    sin = jnp.sin(ang).reshape(bshape)
    xf = x.astype(jnp.float32)
    x1, x2 = xf[..., :half], xf[..., half:]
    return jnp.concatenate([x1 * cos - x2 * sin, x2 * cos + x1 * sin], axis=-1).astype(x.dtype)


def dilated_group_attend(q, k, v, kvalid, window, dil):
    b, tq, h, hd = q.shape
    m = window // dil
    nq = tq // dil
    na = nq + m
    qr = q.reshape(b, nq, dil, h, hd)
    kr = k.reshape(b, na, dil, h, hd)
    vr = v.reshape(b, na, dil, h, hd)
    s = jnp.einsum('bqrhd,barhd->bhrqa', qr, kr, preferred_element_type=jnp.float32) * (hd ** -0.5)
    qa = jnp.arange(nq)[:, None]
    aa = jnp.arange(na)[None, :]
    band = (aa >= qa) & (aa <= qa + m)
    mask = band[None, :, :] & kvalid.reshape(na, dil).T[:, None, :]
    s = jnp.where(mask[None, None], s, NEG)
    lse = jax.nn.logsumexp(s, axis=-1)
    p = jnp.exp(s - lse[..., None])
    o = jnp.einsum('bhrqa,barhd->bqrhd', p.astype(v.dtype), vr).reshape(b, tq, h, hd)
    return o, lse.transpose(0, 3, 2, 1).reshape(b, tq, h)


def combine_by_denominator(outs, lses):
    w = jax.nn.softmax(jnp.stack(lses), axis=0)
    o = jnp.sum(w[..., None] * jnp.stack(outs).astype(jnp.float32), axis=0)
    return o.astype(outs[0].dtype)


def dilated_qkv(h, pos, w_qkv):
    b, t, _ = h.shape
    qkv = (h @ w_qkv).reshape(b, t, N_GROUPS_A, 3, N_HEADS_A, HEAD_DIM)
    q = rope(qkv[:, :, :, 0], pos)
    kv = jnp.stack([rope(qkv[:, :, :, 1], pos), qkv[:, :, :, 2]], axis=3)
    return q, kv


def dilated_prompt(h, w_qkv, w_o):
    b, s, _ = h.shape
    q, kv = dilated_qkv(h, jnp.arange(s), w_qkv)
    padded = [jnp.pad(kv[:, :, g], ((0, 0), (win, 0), (0, 0), (0, 0), (0, 0)))
              for g, (win, _) in enumerate(A_PATTERNS)]

    def block(i):
        t0 = i * Q_BLK
        qb = lax.dynamic_slice_in_dim(q, t0, Q_BLK, axis=1)
        outs, lses = [], []
        for g, (win, dil) in enumerate(A_PATTERNS):
            kb = lax.dynamic_slice_in_dim(padded[g], t0, Q_BLK + win, axis=1)
            valid = t0 + jnp.arange(Q_BLK + win) >= win
            o, l = dilated_group_attend(qb[:, :, g], kb[:, :, 0], kb[:, :, 1], valid, win, dil)
            outs.append(o)
            lses.append(l)
        return combine_by_denominator(outs, lses)

    o = lax.map(block, jnp.arange(s // Q_BLK))
    o = o.transpose(1, 0, 2, 3, 4).reshape(b, s, N_HEADS_A * HEAD_DIM)
    states = tuple(kv[:, s - min(win, s):, g] for g, (win, _) in enumerate(A_PATTERNS))
    return o @ w_o, states


def dilated_sample(h, caches, w_qkv, w_o):
    b, t, _ = h.shape
    q, kv = dilated_qkv(h, PAST_LEN + jnp.arange(t), w_qkv)
    outs, lses, states = [], [], []
    for g, (win, dil) in enumerate(A_PATTERNS):
        cache = caches[g]
        lb = cache.shape[1]
        full = jnp.concatenate([cache, kv[:, :, g]], axis=1)
        tq = -(-t // dil) * dil
        kp = jnp.pad(full, ((0, 0), (win - lb, tq - t), (0, 0), (0, 0), (0, 0)))
        ridx = jnp.arange(win + tq)
        valid = (ridx >= win - lb) & (ridx < win + t)
        qp = jnp.pad(q[:, :, g], ((0, 0), (0, tq - t), (0, 0), (0, 0)))
        o, l = dilated_group_attend(qp, kp[:, :, 0], kp[:, :, 1], valid, win, dil)
        outs.append(o[:, :t])
        lses.append(l[:, :t])
        n_keep = min(win, full.shape[1])
        states.append(full[:, full.shape[1] - n_keep:])
    o = combine_by_denominator(outs, lses).reshape(b, t, N_HEADS_A * HEAD_DIM)
    return o @ w_o, tuple(states)


def nsa_kv_rows(u, pos, w_kv):
    b, t, _ = u.shape
    kv = (u @ w_kv).reshape(b, t, 6, N_KV_B, HEAD_DIM)
    rows = jnp.stack([kv[:, :, 0], kv[:, :, 1], rope(kv[:, :, 2], pos), kv[:, :, 3]], axis=2)
    win = jnp.stack([rope(kv[:, :, 4], pos), kv[:, :, 5]], axis=2)
    return rows, win


def compress(x, pe, w1, b1, w2, b2):
    b, l = x.shape[0], x.shape[1]
    n_c = (l - CMP_LEN) // CMP_STRIDE + 1
    idx = jnp.arange(n_c)[:, None] * CMP_STRIDE + jnp.arange(CMP_LEN)[None, :]
    blk = x[:, idx] + pe[None, None, :, None, :]
    blk = blk.transpose(0, 1, 3, 2, 4).reshape(b, n_c, N_KV_B, CMP_LEN * HEAD_DIM)
    return jax.nn.gelu(blk @ w1 + b1) @ w2 + b2


def nsa_prepare(rows, pe_cmp, w_cmp1, b_cmp1, w_cmp2, b_cmp2):
    b, l = rows.shape[0], rows.shape[1]
    kc = compress(rows[:, :, 0], pe_cmp[0], w_cmp1[0], b_cmp1[0], w_cmp2[0], b_cmp2[0])
    vc = compress(rows[:, :, 1], pe_cmp[1], w_cmp1[1], b_cmp1[1], w_cmp2[1], b_cmp2[1])
    n_s = -(-l // SLC_LEN)
    blocks = jnp.pad(rows[:, :, 2:4], ((0, 0), (0, n_s * SLC_LEN - l), (0, 0), (0, 0), (0, 0)))
    blocks = blocks.reshape(b, n_s, SLC_LEN, 2, N_KV_B, HEAD_DIM).transpose(3, 0, 4, 1, 2, 5)
    return kc, vc, blocks[0], blocks[1]


def nsa_block(q, gates, qpos, kc, vc, ks_blk, vs_blk, kw, vw, kwpos):
    b, tq = q.shape[0], q.shape[1]
    f32 = jnp.float32
    scale = HEAD_DIM ** -0.5
    qg = q.reshape(b, tq, N_KV_B, HPG_B, HEAD_DIM)
    qr = rope(qg, qpos)
    n_c = kc.shape[1]
    c_start = jnp.arange(n_c) * CMP_STRIDE
    c_mask = (c_start + CMP_LEN - 1)[None, :] <= qpos[:, None]
    s_c = jnp.einsum('bqghd,bcgd->bgqhc', qg, kc, preferred_element_type=f32) * scale
    s_c = jnp.where(c_mask[None, None, :, None, :], s_c, NEG)
    p_c = jax.nn.softmax(s_c, axis=-1) * jnp.any(c_mask, axis=-1)[None, None, :, None, None]
    o_c = jnp.einsum('bgqhc,bcgd->bqghd', p_c.astype(vc.dtype), vc)
    n_s = ks_blk.shape[2]
    s_start = jnp.arange(n_s) * SLC_LEN
    cover = ((c_start[:, None] < s_start[None, :] + SLC_LEN)
             & (c_start[:, None] + CMP_LEN > s_start[None, :])).astype(f32)
    imp = jnp.einsum('bgqhc,cs->bgqs', p_c, cover)
    cur = qpos // SLC_LEN
    jb = jnp.arange(n_s)[None, :]
    forced = (jb == 0) | (jb == cur[:, None]) | (jb == cur[:, None] - 1)
    imp = jnp.where(forced, FORCE_SCORE, jnp.where(jb <= cur[:, None], imp, -1.0))
    _, sel = lax.top_k(imp, min(N_SELECT, n_s))
    n_top = sel.shape[-1]
    bi = jnp.arange(b)[:, None, None, None]
    gi = jnp.arange(N_KV_B)[None, :, None, None]
    ksel = ks_blk[bi, gi, sel].reshape(b, N_KV_B, tq, n_top * SLC_LEN, HEAD_DIM)
    vsel = vs_blk[bi, gi, sel].reshape(b, N_KV_B, tq, n_top * SLC_LEN, HEAD_DIM)
    spos = (sel[..., None] * SLC_LEN + jnp.arange(SLC_LEN)).reshape(b, N_KV_B, tq, n_top * SLC_LEN)
    s_s = jnp.einsum('bqghd,bgqkd->bgqhk', qr, ksel, preferred_element_type=f32) * scale
    s_s = jnp.where((spos <= qpos[None, None, :, None])[:, :, :, None, :], s_s, NEG)
    p_s = jax.nn.softmax(s_s, axis=-1)
    o_s = jnp.einsum('bgqhk,bgqkd->bqghd', p_s.astype(vsel.dtype), vsel)
    dist = qpos[:, None] - kwpos[None, :]
    w_mask = (dist >= 0) & (dist <= WIN_B) & (kwpos[None, :] >= 0)
    s_w = jnp.einsum('bqghd,bkgd->bgqhk', qr, kw, preferred_element_type=f32) * scale
    s_w = jnp.where(w_mask[None, None, :, None, :], s_w, NEG)
    p_w = jax.nn.softmax(s_w, axis=-1)
    o_w = jnp.einsum('bgqhk,bkgd->bqghd', p_w.astype(vw.dtype), vw)
    g = gates.reshape(b, tq, N_KV_B, HPG_B, 3)
    o = g[..., 0:1] * o_c + g[..., 1:2] * o_s + g[..., 2:3] * o_w
    return o.reshape(b, tq, N_HEADS_B * HEAD_DIM).astype(q.dtype)


def nsa_query_side(h, w_qg):
    b, t, _ = h.shape
    hq = N_HEADS_B * HEAD_DIM
    qg = h @ w_qg
    q = qg[..., :hq].reshape(b, t, N_HEADS_B, HEAD_DIM)
    gates = jax.nn.sigmoid(qg[..., hq:].astype(jnp.float32)).reshape(b, t, N_HEADS_B, 3)
    return q, gates


def nsa_shared_prompt(u, w_kv, cmp_params):
    s = u.shape[1]
    rows, win = nsa_kv_rows(u, jnp.arange(s), w_kv)
    kc, vc, ksb, vsb = nsa_prepare(rows, *cmp_params)
    win_pad = jnp.pad(win, ((0, 0), (WIN_B, 0), (0, 0), (0, 0), (0, 0)))
    return (kc, vc, ksb, vsb, win_pad), (rows, win[:, s - min(WIN_B, s):])


def nsa_prompt_layer(h, shared, w_qg, w_o):
    kc, vc, ksb, vsb, win_pad = shared
    b, s, _ = h.shape
    q, gates = nsa_query_side(h, w_qg)

    def block(i):
        t0 = i * NSA_Q_BLK
        qb = lax.dynamic_slice_in_dim(q, t0, NSA_Q_BLK, axis=1)
        gb = lax.dynamic_slice_in_dim(gates, t0, NSA_Q_BLK, axis=1)
        wb = lax.dynamic_slice_in_dim(win_pad, t0, NSA_Q_BLK + WIN_B, axis=1)
        qpos = t0 + jnp.arange(NSA_Q_BLK)
        kwpos = t0 - WIN_B + jnp.arange(NSA_Q_BLK + WIN_B)
        return nsa_block(qb, gb, qpos, kc, vc, ksb, vsb, wb[:, :, 0], wb[:, :, 1], kwpos)

    o = lax.map(block, jnp.arange(s // NSA_Q_BLK))
    return o.transpose(1, 0, 2, 3).reshape(b, s, N_HEADS_B * HEAD_DIM) @ w_o


def nsa_shared_sample(u, pool, win_cache, page_table, w_kv, cmp_params):
    b, t, _ = u.shape
    rows, win = nsa_kv_rows(u, PAST_LEN + jnp.arange(t), w_kv)
    past = pool[page_table].reshape(b, -1, N_KV_ROWS, N_KV_B, HEAD_DIM)
    full = jnp.concatenate([past, rows], axis=1)
    kc, vc, ksb, vsb = nsa_prepare(full, *cmp_params)
    lbw = win_cache.shape[1]
    win_full = jnp.concatenate([win_cache, win], axis=1)
    win_pos = PAST_LEN - lbw + jnp.arange(lbw + t)
    n_keep = min(WIN_B, win_full.shape[1])
    return (kc, vc, ksb, vsb, win_full, win_pos), (rows, win_full[:, win_full.shape[1] - n_keep:])


def nsa_sample_layer(h, shared, w_qg, w_o):
    kc, vc, ksb, vsb, win_full, win_pos = shared
    b, t, _ = h.shape
    q, gates = nsa_query_side(h, w_qg)

    def one(i):
        qb = lax.dynamic_slice_in_dim(q, i, 1, axis=1)
        gb = lax.dynamic_slice_in_dim(gates, i, 1, axis=1)
        qpos = (PAST_LEN + i)[None]
        return nsa_block(qb, gb, qpos, kc, vc, ksb, vsb, win_full[:, :, 0], win_full[:, :, 1], win_pos)

    o = lax.map(one, jnp.arange(t))
    return o[:, :, 0].transpose(1, 0, 2) @ w_o


def run_trunk(x, c, mix_a, make_shared, mix_b, g_norm, w_ada, b_ada, g_kv, w_ada_kv, b_ada_kv,
              w_ffn_in, w_ffn_out, g_final):
    a_states = []
    shared, shared_state = None, None
    for layer in range(DEPTH):
        mod = adaln(c, w_ada[layer], b_ada[layer], 6)
        if layer == N_A_LAYERS:
            mkv = adaln(c, w_ada_kv, b_ada_kv, 2)
            u = modulate(rms_norm(x, g_kv), mkv[:, 0], mkv[:, 1])
            shared, shared_state = make_shared(u)
        h = modulate(rms_norm(x, g_norm[layer, 0]), mod[:, 0], mod[:, 1])
        if layer < N_A_LAYERS:
            y, st = mix_a(layer, h)
            a_states.append(st)
        else:
            y = mix_b(layer - N_A_LAYERS, h, shared)
        x = x + mod[:, 2] * y
        h = modulate(rms_norm(x, g_norm[layer, 1]), mod[:, 3], mod[:, 4])
        x = x + mod[:, 5] * swiglu(h, w_ffn_in[layer], w_ffn_out[layer])
    a_stacked = [jnp.stack([st[g] for st in a_states]) for g in range(N_GROUPS_A)]
    return rms_norm(x, g_final), a_stacked, shared_state


def setup_inputs(seed: int = 0) -> dict:
    key = jax.random.key(seed)
    keys = iter(jax.random.split(key, 32))

    def nrm(shape, scale):
        return jax.random.normal(next(keys), shape, jnp.float32) * scale

    d = D_MODEL
    n_pages = PAST_LEN // PAGE_SIZE
    n_pool = (DEC_BATCH * n_pages * 5) // 4
    hq_a = N_HEADS_A * HEAD_DIM
    hq_b = N_HEADS_B * HEAD_DIM
    x_prompt = nrm((BATCH, SEQ, d), 1.0)
    x_sample = nrm((DEC_BATCH, DEC_SEQ, d), 1.0)
    c_prompt = nrm((BATCH, d), 1.0)
    c_sample = nrm((DEC_BATCH, d), 1.0)
    cache_a0 = nrm((N_A_LAYERS, DEC_BATCH, min(A_PATTERNS[0][0], PAST_LEN), 2, N_HEADS_A, HEAD_DIM), 1.0)
    cache_a1 = nrm((N_A_LAYERS, DEC_BATCH, min(A_PATTERNS[1][0], PAST_LEN), 2, N_HEADS_A, HEAD_DIM), 1.0)
    cache_a2 = nrm((N_A_LAYERS, DEC_BATCH, min(A_PATTERNS[2][0], PAST_LEN), 2, N_HEADS_A, HEAD_DIM), 1.0)
    cache_b_pool = nrm((n_pool, PAGE_SIZE, N_KV_ROWS, N_KV_B, HEAD_DIM), 1.0)
    cache_b_win = nrm((DEC_BATCH, min(WIN_B, PAST_LEN), 2, N_KV_B, HEAD_DIM), 1.0)
    perm = jax.random.permutation(next(keys), n_pool)
    page_table = perm[:DEC_BATCH * n_pages].reshape(DEC_BATCH, n_pages).astype(jnp.int32)
    return {
        'x_prompt': x_prompt,
        'x_sample': x_sample,
        'c_prompt': c_prompt,
        'c_sample': c_sample,
        'cache_a0': cache_a0,
        'cache_a1': cache_a1,
        'cache_a2': cache_a2,
        'cache_b_pool': cache_b_pool,
        'cache_b_win': cache_b_win,
        'page_table': page_table,
        'g_norm': 1.0 + nrm((DEPTH, 2, d), 0.05),
        'w_ada': nrm((DEPTH, d, 6 * d), 0.5 * d ** -0.5),
        'b_ada': nrm((DEPTH, 6 * d), 0.02),
        'w_qkv_a': nrm((N_A_LAYERS, d, N_GROUPS_A * 3 * hq_a), d ** -0.5),
        'w_o_a': nrm((N_A_LAYERS, hq_a, d), hq_a ** -0.5),
        'g_kv': 1.0 + nrm((d,), 0.05),
        'w_ada_kv': nrm((d, 2 * d), 0.5 * d ** -0.5),
        'b_ada_kv': nrm((2 * d,), 0.02),
        'w_kv_b': nrm((d, 6 * N_KV_B * HEAD_DIM), d ** -0.5),
        'pe_cmp': nrm((2, CMP_LEN, HEAD_DIM), 0.5),
        'w_cmp1': nrm((2, CMP_LEN * HEAD_DIM, CMP_HIDDEN), (CMP_LEN * HEAD_DIM) ** -0.5),
        'b_cmp1': nrm((2, CMP_HIDDEN), 0.02),
        'w_cmp2': nrm((2, CMP_HIDDEN, HEAD_DIM), CMP_HIDDEN ** -0.5),
        'b_cmp2': nrm((2, HEAD_DIM), 0.02),
        'w_qg_b': nrm((N_B_LAYERS, d, hq_b + 3 * N_HEADS_B), d ** -0.5),
        'w_o_b': nrm((N_B_LAYERS, hq_b, d), hq_b ** -0.5),
        'w_ffn_in': nrm((DEPTH, d, 2 * D_FF), d ** -0.5),
        'w_ffn_out': nrm((DEPTH, D_FF, d), D_FF ** -0.5),
        'g_final': 1.0 + nrm((d,), 0.05),
    }


def reference(x_prompt, x_sample, c_prompt, c_sample, cache_a0, cache_a1, cache_a2, cache_b_pool,
              cache_b_win, page_table, g_norm, w_ada, b_ada, w_qkv_a, w_o_a, g_kv, w_ada_kv, b_ada_kv,
              w_kv_b, pe_cmp, w_cmp1, b_cmp1, w_cmp2, b_cmp2, w_qg_b, w_o_b, w_ffn_in, w_ffn_out, g_final):
    cmp_params = (pe_cmp, w_cmp1, b_cmp1, w_cmp2, b_cmp2)

    def a_prompt(l, h):
        return dilated_prompt(h, w_qkv_a[l], w_o_a[l])

    def shared_prompt(u):
        return nsa_shared_prompt(u, w_kv_b, cmp_params)

    def b_prompt(l, h, sh):
        return nsa_prompt_layer(h, sh, w_qg_b[l], w_o_b[l])

    y_prompt, a_p, (rows_p, win_p) = run_trunk(
        x_prompt, c_prompt, a_prompt, shared_prompt, b_prompt, g_norm, w_ada, b_ada,
        g_kv, w_ada_kv, b_ada_kv, w_ffn_in, w_ffn_out, g_final)

    def a_sample(l, h):
        return dilated_sample(h, (cache_a0[l], cache_a1[l], cache_a2[l]), w_qkv_a[l], w_o_a[l])

    def shared_sample(u):
        return nsa_shared_sample(u, cache_b_pool, cache_b_win, page_table, w_kv_b, cmp_params)

    def b_sample(l, h, sh):
        return nsa_sample_layer(h, sh, w_qg_b[l], w_o_b[l])

    y_sample, a_s, (rows_s, win_s) = run_trunk(
        x_sample, c_sample, a_sample, shared_sample, b_sample, g_norm, w_ada, b_ada,
        g_kv, w_ada_kv, b_ada_kv, w_ffn_in, w_ffn_out, g_final)

    return (y_prompt, y_sample, a_p[0], a_s[0], a_p[1], a_s[1], a_p[2], a_s[2], rows_p, rows_s, win_p, win_s)
```

```python
import functools

import jax
import jax.numpy as jnp
from jax import lax
from jax.experimental import pallas as pl
from jax.experimental.pallas import tpu as pltpu

F32 = jnp.float32
BF16 = jnp.bfloat16

D_MODEL = 1024
HEAD_DIM = 64
N_HEADS = 16
A_PATTERNS = ((128, 1), (512, 4), (2048, 16))
N_KV_B = 4
HPG_B = 4
CMP_LEN = 32
CMP_STRIDE = 16
SLC_LEN = 64
N_SELECT = 16
WIN_B = 512
PAGE = 128
FORCE_SCORE = 1.0e4
ROPE_THETA = 10000.0
EPS = 1e-6
NEG = -1e30
Q_SCALE = HEAD_DIM ** -0.5
LANES = 128


def _cparams(sem, vmem_mb=48):
    return pltpu.CompilerParams(dimension_semantics=sem, vmem_limit_bytes=vmem_mb << 20)


def _dot(a, b):
    return jnp.dot(a, b, preferred_element_type=F32)


def _dot_nt(a, b):
    return lax.dot_general(a, b, (((1,), (1,)), ((), ())), preferred_element_type=F32)


def _split3(x):
    hi = x.astype(BF16)
    r1 = x - hi.astype(F32)
    mid = r1.astype(BF16)
    lo = (r1 - mid.astype(F32)).astype(BF16)
    return hi, mid, lo


def _dot3(x, m01):
    hi, mid, lo = _split3(x)
    return _dot(hi, m01) + _dot(mid, m01) + _dot(lo, m01)


def _norm_mod(x, g, sh, sc):
    ms = jnp.mean(x * x, axis=-1, keepdims=True)
    y = x * lax.rsqrt(ms + EPS) * g
    return y * (1.0 + sc) + sh


def _rope_tile(a, cos, sin):
    tn = a.shape[1]
    reps = tn // LANES
    if reps > 1:
        cos = jnp.concatenate([cos] * reps, axis=1)
        sin = jnp.concatenate([sin] * reps, axis=1)
    lane = lax.broadcasted_iota(jnp.int32, a.shape, 1)
    first = (lane % HEAD_DIM) < (HEAD_DIM // 2)
    rot = jnp.where(first, pltpu.roll(a, tn - HEAD_DIM // 2, 1), pltpu.roll(a, HEAD_DIM // 2, 1))
    return a * cos + rot * sin


def _rope_tables(pos):
    half = HEAD_DIM // 2
    inv_freq = ROPE_THETA ** (-jnp.arange(half, dtype=F32) / half)
    ang = pos.astype(F32)[:, None] * inv_freq[None, :]
    c, s = jnp.cos(ang), jnp.sin(ang)
    cos = jnp.tile(jnp.concatenate([c, c], axis=1), (1, LANES // HEAD_DIM))
    sin = jnp.tile(jnp.concatenate([-s, s], axis=1), (1, LANES // HEAD_DIM))
    return cos, sin


def _ada_kernel(c_ref, w_ref, b_ref, o_ref):
    c = c_ref[...]
    s = (c * jax.nn.sigmoid(c)).astype(BF16)
    o_ref[...] = _dot(s, w_ref[...]) + b_ref[...]


def _adaln(c, w, b, tn):
    mp, d = c.shape
    n_l, _, n = w.shape
    return pl.pallas_call(
        _ada_kernel,
        grid=(n_l, n // tn),
        in_specs=[pl.BlockSpec((mp, d), lambda l, j: (0, 0)),
                  pl.BlockSpec((None, d, tn), lambda l, j: (l, 0, j)),
                  pl.BlockSpec((None, 1, tn), lambda l, j: (l, 0, j))],
        out_specs=pl.BlockSpec((None, mp, tn), lambda l, j: (l, 0, j)),
        out_shape=jax.ShapeDtypeStruct((n_l, mp, n), F32),
        compiler_params=_cparams(("arbitrary", "arbitrary")),
        name="adaln",
    )(c, w, b)


def _proj_kernel(x_ref, sh_ref, sc_ref, g_ref, w_ref, cos_ref, sin_ref, *rest, mode, tn, rope_rule):
    n_out = 2 if mode == "both" else 1
    out_refs = rest[:n_out]
    h_scr = rest[n_out]
    j = pl.program_id(1)

    @pl.when(j == 0)
    def _():
        h_scr[...] = _norm_mod(x_ref[...], g_ref[...], sh_ref[...], sc_ref[...]).astype(BF16)

    acc = _dot(h_scr[...], w_ref[...])
    if mode == "sigmoid":
        out_refs[0][...] = jax.nn.sigmoid(acc)
    elif mode == "both":
        out_refs[0][...] = acc
        out_refs[1][...] = _rope_tile(acc, cos_ref[...], sin_ref[...])
    else:
        period, units = rope_rule
        unit = ((j * tn) // 256) % period
        flag = unit == units[0]
        for u in units[1:]:
            flag = jnp.logical_or(flag, unit == u)

        @pl.when(flag)
        def _():
            out_refs[0][...] = _rope_tile(acc, cos_ref[...], sin_ref[...])

        @pl.when(jnp.logical_not(flag))
        def _():
            out_refs[0][...] = acc


def _proj(x, sh, sc, mod_spec, g, w, cos, sin, tab_spec, *, tm, tn, mode, rope_rule=None, name):
    m, k = x.shape
    n = w.shape[1]
    n_out = 2 if mode == "both" else 1
    kern = functools.partial(_proj_kernel, mode=mode, tn=tn, rope_rule=rope_rule)
    out_spec = pl.BlockSpec((tm, tn), lambda i, j: (i, j))
    out = pl.pallas_call(
        kern,
        grid=(m // tm, n // tn),
        in_specs=[pl.BlockSpec((tm, k), lambda i, j: (i, 0)), mod_spec, mod_spec,
                  pl.BlockSpec((1, k), lambda i, j: (0, 0)),
                  pl.BlockSpec((k, tn), lambda i, j: (0, j)), tab_spec, tab_spec],
        out_specs=[out_spec] * n_out,
        out_shape=[jax.ShapeDtypeStruct((m, n), F32)] * n_out,
        scratch_shapes=[pltpu.VMEM((tm, k), BF16)],
        compiler_params=_cparams(("arbitrary", "arbitrary")),
        name=name,
    )(x, sh, sc, g, w, cos, sin)
    return out if n_out == 2 else out[0]


def _oproj_kernel(*refs, n_grp):
    if n_grp:
        o_refs, l_refs = refs[:n_grp], refs[n_grp:2 * n_grp]
        ls = [r[...] for r in l_refs]
        mx = ls[0]
        for l in ls[1:]:
            mx = jnp.maximum(mx, l)
        es = [jnp.exp(l - mx) for l in ls]
        den = es[0]
        num = es[0] * o_refs[0][...]
        for e, o in zip(es[1:], o_refs[1:]):
            den = den + e
            num = num + e * o[...]
        a = num / den
        rest = refs[2 * n_grp:]
    else:
        a = refs[0][...]
        rest = refs[1:]
    w_ref, x_ref, gt_ref, out_ref = rest
    out_ref[...] = x_ref[...] + gt_ref[...] * _dot(a.astype(BF16), w_ref[...])


def _oproj(acts, w, x, gate, mod_spec, *, tm, n_grp, name):
    m, d = x.shape
    k = w.shape[0]
    row = pl.BlockSpec((tm, k), lambda i: (i, 0))
    return pl.pallas_call(
        functools.partial(_oproj_kernel, n_grp=n_grp),
        grid=(m // tm,),
        in_specs=[row] * len(acts) + [pl.BlockSpec((k, d), lambda i: (0, 0)),
                                      pl.BlockSpec((tm, d), lambda i: (i, 0)), mod_spec],
        out_specs=pl.BlockSpec((tm, d), lambda i: (i, 0)),
        out_shape=jax.ShapeDtypeStruct((m, d), F32),
        compiler_params=_cparams(("arbitrary",)),
        name=name,
    )(*acts, w, x, gate)


def _ffn_kernel(x_ref, sh_ref, sc_ref, gt_ref, g_ref, wa_ref, wg_ref, wo_ref, out_ref, h_scr, acc_scr):
    f = pl.program_id(1)

    @pl.when(f == 0)
    def _():
        h_scr[...] = _norm_mod(x_ref[...], g_ref[...], sh_ref[...], sc_ref[...]).astype(BF16)
        acc_scr[...] = jnp.zeros_like(acc_scr)

    h = h_scr[...]
    a = _dot(h, wa_ref[...])
    gg = _dot(h, wg_ref[...])
    act = (a * jax.nn.sigmoid(a)) * gg
    acc_scr[...] += _dot(act.astype(BF16), wo_ref[...])

    @pl.when(f == pl.num_programs(1) - 1)
    def _():
        out_ref[...] = x_ref[...] + gt_ref[...] * acc_scr[...]


def _ffn(x, sh, sc, gt, mod_spec, g, w_in, w_out, *, tm, tf, name):
    m, d = x.shape
    dff = w_out.shape[0]
    nf = dff // tf
    return pl.pallas_call(
        _ffn_kernel,
        grid=(m // tm, nf),
        in_specs=[pl.BlockSpec((tm, d), lambda i, f: (i, 0)), mod_spec, mod_spec, mod_spec,
                  pl.BlockSpec((1, d), lambda i, f: (0, 0)),
                  pl.BlockSpec((d, tf), lambda i, f: (0, f)),
                  pl.BlockSpec((d, tf), lambda i, f: (0, f + nf)),
                  pl.BlockSpec((tf, d), lambda i, f: (f, 0))],
        out_specs=pl.BlockSpec((tm, d), lambda i, f: (i, 0)),
        out_shape=jax.ShapeDtypeStruct((m, d), F32),
        scratch_shapes=[pltpu.VMEM((tm, d), BF16), pltpu.VMEM((tm, d), F32)],
        compiler_params=_cparams(("arbitrary", "arbitrary")),
        name=name,
    )(x, sh, sc, gt, g, w_in, w_in, w_out)


def _final_norm_kernel(x_ref, g_ref, o_ref):
    x = x_ref[...]
    ms = jnp.mean(x * x, axis=-1, keepdims=True)
    o_ref[...] = x * lax.rsqrt(ms + EPS) * g_ref[...]


def _final_norm(x, g, tm):
    m, d = x.shape
    return pl.pallas_call(
        _final_norm_kernel,
        grid=(m // tm,),
        in_specs=[pl.BlockSpec((tm, d), lambda i: (i, 0)), pl.BlockSpec((1, d), lambda i: (0, 0))],
        out_specs=pl.BlockSpec((tm, d), lambda i: (i, 0)),
        out_shape=jax.ShapeDtypeStruct((m, d), F32),
        compiler_params=_cparams(("arbitrary",)),
        name="final_norm",
    )(x, g)


A_BLK = 128


def _dil_prompt_kernel(q_ref, kp_ref, kc_ref, vp_ref, vc_ref, o_ref, l_ref):
    i = pl.program_id(2)
    row = lax.broadcasted_iota(jnp.int32, (A_BLK, A_BLK), 0)
    col = lax.broadcasted_iota(jnp.int32, (A_BLK, A_BLK), 1)
    m_prev = jnp.logical_and(col >= row, i > 0)
    m_cur = col <= row
    for h in range(N_HEADS):
        sl = slice(h * HEAD_DIM, (h + 1) * HEAD_DIM)
        q = (q_ref[:, sl] * Q_SCALE).astype(BF16)
        s_p = jnp.where(m_prev, _dot_nt(q, kp_ref[:, sl].astype(BF16)), NEG)
        s_c = jnp.where(m_cur, _dot_nt(q, kc_ref[:, sl].astype(BF16)), NEG)
        mx = jnp.maximum(jnp.max(s_p, axis=-1, keepdims=True), jnp.max(s_c, axis=-1, keepdims=True))
        p_p = jnp.exp(s_p - mx)
        p_c = jnp.exp(s_c - mx)
        den = jnp.sum(p_p, axis=-1, keepdims=True) + jnp.sum(p_c, axis=-1, keepdims=True)
        o = _dot(p_p.astype(BF16), vp_ref[:, sl].astype(BF16)) + _dot(p_c.astype(BF16), vc_ref[:, sl].astype(BF16))
        o_ref[:, sl] = o / den
        l_ref[:, sl] = jnp.broadcast_to(mx + jnp.log(den), (A_BLK, HEAD_DIM))


def _dil_prompt(qkv, grp, bsz, t):
    _, dil = A_PATTERNS[grp]
    tu = t // dil
    ncol = qkv.shape[-1] // D_MODEL
    qv = qkv.reshape(bsz, tu, dil * qkv.shape[-1])

    def spec(which, prev):
        cb = 3 * grp + which
        if prev:
            return pl.BlockSpec((None, A_BLK, D_MODEL), lambda b, r, i: (b, jnp.maximum(i - 1, 0), r * ncol + cb))
        return pl.BlockSpec((None, A_BLK, D_MODEL), lambda b, r, i: (b, i, r * ncol + cb))

    out_spec = pl.BlockSpec((None, A_BLK, D_MODEL), lambda b, r, i: (b, i, r))
    shp = jax.ShapeDtypeStruct((bsz, tu, dil * D_MODEL), F32)
    o, l = pl.pallas_call(
        _dil_prompt_kernel,
        grid=(bsz, dil, tu // A_BLK),
        in_specs=[spec(0, False), spec(1, True), spec(1, False), spec(2, True), spec(2, False)],
        out_specs=[out_spec, out_spec],
        out_shape=[shp, shp],
        compiler_params=_cparams(("arbitrary", "arbitrary", "arbitrary")),
        name=f"dil_prompt_g{grp}",
    )(qv, qv, qv, qv, qv)
    return o.reshape(bsz * t, D_MODEL), l.reshape(bsz * t, D_MODEL)


def _head_mask(rows, cols, row_head, col_div):
    r = lax.broadcasted_iota(jnp.int32, (rows, cols), 0)
    c = lax.broadcasted_iota(jnp.int32, (rows, cols), 1)
    return row_head(r) == c // col_div


def _dil_sample_kernel(q_ref, kn_ref, vn_ref, cache_ref, o_ref, l_ref, qx_scr, m_scr, l_scr, acc_scr, *, dil, wc, t_new):
    c = pl.program_id(1)
    nrow = N_HEADS * t_new
    hmask = _head_mask(nrow, D_MODEL, lambda r: r // t_new, HEAD_DIM)

    @pl.when(c == 0)
    def _():
        q = q_ref[...] * Q_SCALE
        qt = jnp.broadcast_to(q[None], (N_HEADS, t_new, D_MODEL)).reshape(nrow, D_MODEL)
        qx_scr[...] = jnp.where(hmask, qt, 0.0).astype(BF16)
        m_scr[...] = jnp.full_like(m_scr, NEG)
        l_scr[...] = jnp.zeros_like(l_scr)
        acc_scr[...] = jnp.zeros_like(acc_scr)

    qx = qx_scr[...]

    def update(s, v):
        m_old = m_scr[...]
        m_new = jnp.maximum(m_old, jnp.max(s, axis=-1, keepdims=True))
        alpha = jnp.exp(m_old - m_new)
        p = jnp.exp(s - m_new)
        l_scr[...] = alpha * l_scr[...] + jnp.sum(p, axis=-1, keepdims=True)
        acc_scr[...] = alpha * acc_scr[...] + _dot(p.astype(BF16), v)
        m_scr[...] = m_new

    k = cache_ref[:, :D_MODEL].astype(BF16)
    v = cache_ref[:, D_MODEL:].astype(BF16)
    qi = lax.broadcasted_iota(jnp.int32, (nrow, wc), 0) % t_new
    rho = c * wc + lax.broadcasted_iota(jnp.int32, (nrow, wc), 1)
    valid = jnp.logical_and((rho - qi) % dil == 0, rho >= qi)
    update(jnp.where(valid, _dot_nt(qx, k), NEG), v)

    @pl.when(c == pl.num_programs(1) - 1)
    def _():
        pad = jnp.zeros((LANES - t_new, D_MODEL), F32)
        kn = jnp.concatenate([kn_ref[...], pad], axis=0).astype(BF16)
        vn = jnp.concatenate([vn_ref[...], pad], axis=0).astype(BF16)
        qi2 = lax.broadcasted_iota(jnp.int32, (nrow, LANES), 0) % t_new
        kj = lax.broadcasted_iota(jnp.int32, (nrow, LANES), 1)
        ok = jnp.logical_and(kj <= qi2, (qi2 - kj) % dil == 0)
        update(jnp.where(ok, _dot_nt(qx, kn), NEG), vn)
        den = l_scr[...]
        res = jnp.where(hmask, acc_scr[...] / den, 0.0)
        lse = jnp.where(hmask, m_scr[...] + jnp.log(den), 0.0)
        o_ref[...] = jnp.sum(res.reshape(N_HEADS, t_new, D_MODEL), axis=0)
        l_ref[...] = jnp.sum(lse.reshape(N_HEADS, t_new, D_MODEL), axis=0)


def _dil_sample(qkv, cache, layer, grp, nb, t_new):
    win, dil = A_PATTERNS[grp]
    wc = min(win, 512)
    ncol = qkv.shape[-1] // D_MODEL
    qv = qkv.reshape(nb, t_new, qkv.shape[-1])
    nrow = N_HEADS * t_new

    def new_spec(which):
        cb = 3 * grp + which
        return pl.BlockSpec((None, t_new, D_MODEL), lambda b, c: (b, 0, cb))

    out_spec = pl.BlockSpec((None, t_new, D_MODEL), lambda b, c: (b, 0, 0))
    shp = jax.ShapeDtypeStruct((nb, t_new, D_MODEL), F32)
    o, l = pl.pallas_call(
        functools.partial(_dil_sample_kernel, dil=dil, wc=wc, t_new=t_new),
        grid=(nb, win // wc),
        in_specs=[new_spec(0), new_spec(1), new_spec(2),
                  pl.BlockSpec((None, None, wc, 2 * D_MODEL), lambda b, c: (layer, b, c, 0))],
        out_specs=[out_spec, out_spec],
        out_shape=[shp, shp],
        scratch_shapes=[pltpu.VMEM((nrow, D_MODEL), BF16), pltpu.VMEM((nrow, 1), F32),
                        pltpu.VMEM((nrow, 1), F32), pltpu.VMEM((nrow, D_MODEL), F32)],
        compiler_params=_cparams(("arbitrary", "arbitrary")),
        name=f"dil_sample_g{grp}",
    )(qv, qv, qv, cache)
    return o.reshape(nb * t_new, D_MODEL), l.reshape(nb * t_new, D_MODEL)


CMP_SUB = PAGE // CMP_STRIDE


def _compress_kernel(pt_ref, pg0_ref, pg1_ref, pg2_ref, pg3_ref, pelo_ref, pehi_ref, w1lo_ref, w1hi_ref, b1_ref,
                     w2_ref, b2_ref, kc_ref, vc_ref, zlo, zhi):
    del pt_ref
    k = pl.program_id(1)
    npages = pl.num_programs(1)
    nu = zlo.shape[1]
    r0 = pl.multiple_of(k * CMP_SUB, CMP_SUB)
    for m, page_ref in enumerate((pg0_ref, pg1_ref, pg2_ref, pg3_ref)):
        for p in range(CMP_STRIDE):
            rows = page_ref[pl.ds(p, CMP_SUB, stride=CMP_STRIDE), :]
            zlo[m, pl.ds(r0, CMP_SUB), p * LANES:(p + 1) * LANES] = rows + pelo_ref[p:p + 1, m * LANES:(m + 1) * LANES]
            zhi[m, pl.ds(r0, CMP_SUB), p * LANES:(p + 1) * LANES] = rows + pehi_ref[p:p + 1, m * LANES:(m + 1) * LANES]

    @pl.when(k == npages - 1)
    def _():
        for m in range(4):
            ty = m // 2
            first = _dot(zlo[m].astype(BF16), w1lo_ref[ty])
            second = _dot(zhi[m].astype(BF16), w1hi_ref[ty])
            hid = first + pltpu.roll(second, nu - 1, 0) + b1_ref[ty]
            out = _dot(jax.nn.gelu(hid).astype(BF16), w2_ref[ty]) + b2_ref[ty]
            dst = kc_ref if ty == 0 else vc_ref
            dst[:, (m % 2) * LANES:(m % 2 + 1) * LANES] = out


def _compress(pages, table, nb, npages, cw):
    pelo, pehi, w1lo, w1hi, b1, w2, b2 = cw
    nu = npages * CMP_SUB
    const2 = lambda shape: pl.BlockSpec(shape, lambda b, k, pt: (0,) * len(shape))

    def chunk_spec(m):
        return pl.BlockSpec((None, PAGE, LANES), lambda b, k, pt: (pt[b * npages + k], 0, m))

    grid_spec = pltpu.PrefetchScalarGridSpec(
        num_scalar_prefetch=1,
        grid=(nb, npages),
        in_specs=[chunk_spec(m) for m in range(4)]
                 + [const2(pelo.shape), const2(pehi.shape), const2(w1lo.shape), const2(w1hi.shape),
                    const2(b1.shape), const2(w2.shape), const2(b2.shape)],
        out_specs=[pl.BlockSpec((None, nu, N_KV_B * HEAD_DIM), lambda b, k, pt: (b, 0, 0))] * 2,
        scratch_shapes=[pltpu.VMEM((4, nu, CMP_STRIDE * LANES), F32)] * 2,
    )
    shp = jax.ShapeDtypeStruct((nb, nu, N_KV_B * HEAD_DIM), F32)
    return pl.pallas_call(
        _compress_kernel, grid_spec=grid_spec, out_shape=[shp, shp],
        compiler_params=_cparams(("arbitrary", "arbitrary")), name="nsa_compress",
    )(table, pages, pages, pages, pages, pelo, pehi, w1lo, w1hi, b1, w2, b2)


def _compress_weights(pe_cmp, w_cmp1, b_cmp1, w_cmp2, b_cmp2):
    eye2 = jnp.eye(2, dtype=F32)
    hid = w_cmp1.shape[-1]
    w1 = w_cmp1.reshape(2, CMP_LEN, HEAD_DIM, hid)

    def pair_w1(w):
        return jnp.einsum("pen,ab->paebn", w, eye2).reshape(CMP_STRIDE * LANES, 2 * hid)

    w1lo = jnp.stack([pair_w1(w1[ty, :CMP_STRIDE]) for ty in range(2)]).astype(BF16)
    w1hi = jnp.stack([pair_w1(w1[ty, CMP_STRIDE:]) for ty in range(2)]).astype(BF16)
    w2 = jnp.stack([jnp.einsum("ne,ab->anbe", w_cmp2[ty], eye2).reshape(2 * hid, 2 * HEAD_DIM)
                    for ty in range(2)]).astype(BF16)
    b1 = jnp.tile(b_cmp1, (1, 2))[:, None, :]
    b2 = jnp.tile(b_cmp2, (1, 2))[:, None, :]
    pelo = jnp.concatenate([jnp.tile(pe_cmp[ty, :CMP_STRIDE], (1, N_KV_B)) for ty in range(2)], axis=1)
    pehi = jnp.concatenate([jnp.tile(pe_cmp[ty, CMP_STRIDE:], (1, N_KV_B)) for ty in range(2)], axis=1)
    return pelo, pehi, w1lo, w1hi, b1, w2, b2


C_TQ = 128


def _rank_select(imp_t, n_blk):
    sidx = lax.broadcasted_iota(jnp.int32, imp_t.shape, 0)
    cnt = jnp.zeros(imp_t.shape, jnp.int32)
    for s in range(n_blk):
        row = imp_t[s:s + 1, :]
        ahead = jnp.logical_or(row > imp_t, jnp.logical_and(row == imp_t, sidx > s))
        cnt = cnt + ahead.astype(jnp.int32)
    return cnt < N_SELECT


def _cmp_prompt_kernel(q_ref, gates_ref, kc_ref, vc_ref, cover_ref, ec_ref, oc_ref, sb_ref):
    i = pl.program_id(1)
    n_c = kc_ref.shape[0]
    n_s = sb_ref.shape[-1]
    t = i * C_TQ + lax.broadcasted_iota(jnp.int32, (C_TQ, n_c), 0)
    c_idx = lax.broadcasted_iota(jnp.int32, (C_TQ, n_c), 1)
    valid = c_idx * CMP_STRIDE + (CMP_LEN - 1) <= t
    any_valid = (t[:, :1] >= CMP_LEN - 1).astype(F32)
    tq = i * C_TQ + lax.broadcasted_iota(jnp.int32, (C_TQ, LANES), 0)
    jb = lax.broadcasted_iota(jnp.int32, (C_TQ, LANES), 1)
    cur = tq // SLC_LEN
    forced = jnp.logical_or(jb == 0, jnp.logical_or(jb == cur, jb == cur - 1))
    cover = cover_ref[...]
    gates = gates_ref[...]
    gh = gates.astype(BF16)
    gl = (gates - gh.astype(F32)).astype(BF16)
    gfull = _dot(gh, ec_ref[...]) + _dot(gl, ec_ref[...])
    for g in range(N_KV_B):
        gs = slice(g * HEAD_DIM, (g + 1) * HEAD_DIM)
        kc = kc_ref[:, gs].astype(BF16)
        vc = vc_ref[:, gs].astype(BF16)
        psum = jnp.zeros((C_TQ, n_c), F32)
        for j in range(HPG_B):
            h = g * HPG_B + j
            hs = slice(h * HEAD_DIM, (h + 1) * HEAD_DIM)
            q = (q_ref[:, hs] * Q_SCALE).astype(BF16)
            s = jnp.where(valid, _dot_nt(q, kc), NEG)
            mx = jnp.max(s, axis=-1, keepdims=True)
            e = jnp.exp(s - mx)
            p = e / jnp.sum(e, axis=-1, keepdims=True) * any_valid
            psum = psum + p
            oc_ref[:, hs] = _dot(p.astype(BF16), vc) * gfull[:, hs]
        imp = _dot3(psum, cover)
        imp = jnp.where(forced, FORCE_SCORE, jnp.where(jb <= cur, imp, -1.0))
        sel_t = _rank_select(imp.T[:n_s, :], n_s)
        bias_t = jnp.where(sel_t, 0.0, NEG)
        bias_t = jnp.concatenate([bias_t, jnp.zeros((LANES - n_s, C_TQ), F32)], axis=0)
        sb_ref[g] = bias_t.T[:, :n_s].astype(BF16)


def _cmp_prompt(q, gates, kc, vc, cover, e_c, bsz, t):
    n_c = kc.shape[1]
    n_s = t // SLC_LEN
    qv = q.reshape(bsz, t, D_MODEL)
    gv = gates.reshape(bsz, t, LANES)
    oc, sb = pl.pallas_call(
        _cmp_prompt_kernel,
        grid=(bsz, t // C_TQ),
        in_specs=[pl.BlockSpec((None, C_TQ, D_MODEL), lambda b, i: (b, i, 0)),
                  pl.BlockSpec((None, C_TQ, LANES), lambda b, i: (b, i, 0)),
                  pl.BlockSpec((None, n_c, N_KV_B * HEAD_DIM), lambda b, i: (b, 0, 0)),
                  pl.BlockSpec((None, n_c, N_KV_B * HEAD_DIM), lambda b, i: (b, 0, 0)),
                  pl.BlockSpec(cover.shape, lambda b, i: (0, 0)),
                  pl.BlockSpec(e_c.shape, lambda b, i: (0, 0))],
        out_specs=[pl.BlockSpec((None, C_TQ, D_MODEL), lambda b, i: (b, i, 0)),
                   pl.BlockSpec((None, N_KV_B, C_TQ, n_s), lambda b, i: (b, 0, i, 0))],
        out_shape=[jax.ShapeDtypeStruct((bsz, t, D_MODEL), F32),
                   jax.ShapeDtypeStruct((bsz, N_KV_B, t, n_s), BF16)],
        compiler_params=_cparams(("arbitrary", "arbitrary")),
        name="nsa_cmp_prompt",
    )(qv, gv, kc, vc, cover, e_c)
    return oc, sb


S_TQ = 64
S_TK = 256
W_TK = 192


def _selwin_prompt_kernel(qr_ref, sb_ref, gates_ref, oc_ref, ka_ref, vs_ref, kw_ref, vw_ref, es_ref, ew_ref, o_ref):
    i = pl.program_id(2)
    t0 = i * S_TQ
    nrow = HPG_B * S_TQ
    qr = qr_ref[...] * Q_SCALE
    qa = jnp.concatenate([qr[:, j * HEAD_DIM:(j + 1) * HEAD_DIM] for j in range(HPG_B)], axis=0).astype(BF16)
    bias = sb_ref[...]
    q_aug = jnp.concatenate([qa, jnp.concatenate([bias] * HPG_B, axis=0)], axis=1)
    tpos = t0 + lax.broadcasted_iota(jnp.int32, (nrow, 1), 0) % S_TQ

    def flash(n_tiles, tile_fn):
        def body(kt, carry):
            m_old, l_old, acc = carry
            s, v = tile_fn(kt)
            m_new = jnp.maximum(m_old, jnp.max(s, axis=-1, keepdims=True))
            alpha = jnp.exp(m_old - m_new)
            p = jnp.exp(s - m_new)
            l_new = alpha * l_old + jnp.sum(p, axis=-1, keepdims=True)
            return m_new, l_new, alpha * acc + _dot(p.astype(BF16), v)

        init = (jnp.full((nrow, 1), NEG, F32), jnp.zeros((nrow, 1), F32), jnp.zeros((nrow, HEAD_DIM), F32))
        _, l_fin, acc = lax.fori_loop(0, n_tiles, body, init)
        return acc / l_fin

    def sel_tile(kt):
        k0 = pl.multiple_of(kt * S_TK, S_TK)
        s = _dot_nt(q_aug, ka_ref[pl.ds(k0, S_TK), :])
        kpos = k0 + lax.broadcasted_iota(jnp.int32, (nrow, S_TK), 1)
        return jnp.where(kpos <= tpos, s, NEG), vs_ref[pl.ds(k0, S_TK), :]

    def win_tile(kt):
        k0 = pl.multiple_of(t0 + kt * W_TK, SLC_LEN)
        s = _dot_nt(qa, kw_ref[pl.ds(k0, W_TK), :])
        kpos = k0 - WIN_B + lax.broadcasted_iota(jnp.int32, (nrow, W_TK), 1)
        ok = jnp.logical_and(kpos >= 0, jnp.logical_and(kpos <= tpos, tpos - kpos <= WIN_B))
        return jnp.where(ok, s, NEG), vw_ref[pl.ds(k0, W_TK), :]

    o_s = flash((t0 + S_TQ - 1) // S_TK + 1, sel_tile)
    o_w = flash((WIN_B + S_TQ) // W_TK, win_tile)

    gates = gates_ref[...]
    gh = gates.astype(BF16)
    gl = (gates - gh.astype(F32)).astype(BF16)
    g_s = _dot(gh, es_ref[...]) + _dot(gl, es_ref[...])
    g_w = _dot(gh, ew_ref[...]) + _dot(gl, ew_ref[...])
    for j in range(HPG_B):
        hs = slice(j * HEAD_DIM, (j + 1) * HEAD_DIM)
        rs = slice(j * S_TQ, (j + 1) * S_TQ)
        o_ref[:, hs] = oc_ref[:, hs] + g_s[:, hs] * o_s[rs, :] + g_w[:, hs] * o_w[rs, :]


def _selwin_prompt(qr, sb, gates, oc, ka, vs, kw, vw, e_s, e_w, bsz, t):
    gw = HPG_B * HEAD_DIM
    qv = qr.reshape(bsz, t, D_MODEL)
    gv = gates.reshape(bsz, t, LANES)
    n_s = sb.shape[-1]
    tp = kw.shape[2]
    tok = pl.BlockSpec((None, S_TQ, gw), lambda b, g, i: (b, i, g))
    return pl.pallas_call(
        _selwin_prompt_kernel,
        grid=(bsz, N_KV_B, t // S_TQ),
        in_specs=[tok,
                  pl.BlockSpec((None, None, S_TQ, n_s), lambda b, g, i: (b, g, i, 0)),
                  pl.BlockSpec((None, S_TQ, LANES), lambda b, g, i: (b, i, 0)),
                  tok,
                  pl.BlockSpec((None, None, t, 2 * HEAD_DIM), lambda b, g, i: (b, g, 0, 0)),
                  pl.BlockSpec((None, None, t, HEAD_DIM), lambda b, g, i: (b, g, 0, 0)),
                  pl.BlockSpec((None, None, tp, HEAD_DIM), lambda b, g, i: (b, g, 0, 0)),
                  pl.BlockSpec((None, None, tp, HEAD_DIM), lambda b, g, i: (b, g, 0, 0)),
                  pl.BlockSpec((LANES, gw), lambda b, g, i: (0, g)),
                  pl.BlockSpec((LANES, gw), lambda b, g, i: (0, g))],
        out_specs=tok,
        out_shape=jax.ShapeDtypeStruct((bsz, t, D_MODEL), F32),
        compiler_params=_cparams(("arbitrary", "arbitrary", "arbitrary")),
        name="nsa_selwin_prompt",
    )(qv, sb, gv, oc, ka, vs, kw, vw, e_s, e_w)


def _nsa_sample_kernel(pt_ref, q_ref, qr_ref, gates_ref, kc_ref, vc_ref, newrows_ref, wcache_ref, wnew_ref,
                       fold_ref, foldt_ref, cover_ref, blk1h_ref, *rest, npages, t_new, n_c_valid, n_s_valid):
    del pt_ref
    page_refs = rest[:npages]
    o_ref = rest[npages]
    nrow = N_HEADS * t_new
    gw = N_KV_B * HEAD_DIM
    row_head = lambda r: ((r // t_new) % N_KV_B) * HPG_B + r // (t_new * N_KV_B)
    hmask = _head_mask(nrow, D_MODEL, row_head, HEAD_DIM)
    fold = fold_ref[...]

    def qexp(ref):
        q = ref[...] * Q_SCALE
        qt = jnp.broadcast_to(q[None], (N_HEADS, t_new, D_MODEL)).reshape(nrow, D_MODEL)
        return _dot(jnp.where(hmask, qt, 0.0).astype(BF16), fold).astype(BF16)

    qc = qexp(q_ref)
    qr = qexp(qr_ref)
    qi = lax.broadcasted_iota(jnp.int32, (nrow, LANES), 0) % t_new
    lane = lax.broadcasted_iota(jnp.int32, (nrow, LANES), 1)
    new_ok = lane <= qi

    def pad_new(x):
        return jnp.concatenate([x, jnp.zeros((LANES - t_new, x.shape[1]), F32)], axis=0).astype(BF16)

    kc = kc_ref[...].astype(BF16)
    s = jnp.where(lane < n_c_valid, _dot_nt(qc, kc), NEG)
    e = jnp.exp(s - jnp.max(s, axis=-1, keepdims=True))
    p_c = e / jnp.sum(e, axis=-1, keepdims=True)
    res_c = _dot(p_c.astype(BF16), vc_ref[...].astype(BF16))

    ng = N_KV_B * t_new
    psum = p_c[0:ng] + p_c[ng:2 * ng] + p_c[2 * ng:3 * ng] + p_c[3 * ng:4 * ng]
    imp = _dot3(psum, cover_ref[...])
    jb = lax.broadcasted_iota(jnp.int32, (ng, LANES), 1)
    cur = n_s_valid - 1
    forced = jnp.logical_or(jb == 0, jnp.logical_or(jb == cur, jb == cur - 1))
    imp = jnp.where(forced, FORCE_SCORE, jnp.where(jb <= cur, imp, -2.0))
    cnt = jnp.zeros((ng, LANES), jnp.int32)
    for sblk in range(n_s_valid):
        colv = jnp.broadcast_to(imp[:, sblk:sblk + 1], (ng, LANES))
        ahead = jnp.logical_or(colv > imp, jnp.logical_and(colv == imp, jb > sblk))
        cnt = cnt + ahead.astype(jnp.int32)
    bias_g = jnp.where(cnt < N_SELECT, 0.0, NEG).astype(BF16)
    bias = jnp.concatenate([bias_g] * HPG_B, axis=0)
    key_bias = _dot(bias, blk1h_ref[...])

    past = npages * PAGE
    s_parts = [_dot_nt(qr, page_refs[k][:, :gw].astype(BF16)) for k in range(npages)]
    s_new = jnp.where(new_ok, _dot_nt(qr, pad_new(newrows_ref[:, :gw])), NEG)
    s_all = jnp.concatenate(s_parts + [s_new], axis=1) + key_bias
    mx = jnp.max(s_all, axis=-1, keepdims=True)
    p_s = jnp.exp(s_all - mx)
    den_s = jnp.sum(p_s, axis=-1, keepdims=True)
    p_sb = p_s.astype(BF16)
    res_s = _dot(p_sb[:, past:], pad_new(newrows_ref[:, gw:]))
    for k in range(npages):
        res_s = res_s + _dot(p_sb[:, k * PAGE:(k + 1) * PAGE], page_refs[k][:, gw:].astype(BF16))
    res_s = res_s / den_s

    nw = wcache_ref.shape[0]
    qiw = lax.broadcasted_iota(jnp.int32, (nrow, nw), 0) % t_new
    rw = lax.broadcasted_iota(jnp.int32, (nrow, nw), 1)
    s_w = jnp.where(rw >= qiw + (nw - WIN_B), _dot_nt(qr, wcache_ref[:, :gw].astype(BF16)), NEG)
    s_wn = jnp.where(new_ok, _dot_nt(qr, pad_new(wnew_ref[:, :gw])), NEG)
    mxw = jnp.maximum(jnp.max(s_w, axis=-1, keepdims=True), jnp.max(s_wn, axis=-1, keepdims=True))
    p_w = jnp.exp(s_w - mxw)
    p_wn = jnp.exp(s_wn - mxw)
    den_w = jnp.sum(p_w, axis=-1, keepdims=True) + jnp.sum(p_wn, axis=-1, keepdims=True)
    res_w = (_dot(p_w.astype(BF16), wcache_ref[:, gw:].astype(BF16))
             + _dot(p_wn.astype(BF16), pad_new(wnew_ref[:, gw:]))) / den_w

    gt = jnp.broadcast_to(gates_ref[...][None], (N_HEADS, t_new, LANES)).reshape(nrow, LANES)
    rh = row_head(lax.broadcasted_iota(jnp.int32, (nrow, LANES), 0))

    def gate(branch):
        return jnp.sum(jnp.where(lane == 3 * rh + branch, gt, 0.0), axis=-1, keepdims=True)

    tot = gate(0) * res_c + gate(1) * res_s + gate(2) * res_w
    wide = jnp.where(hmask, _dot3(tot, foldt_ref[...]), 0.0)
    o_ref[...] = jnp.sum(wide.reshape(N_HEADS, t_new, D_MODEL), axis=0)


def _nsa_sample(q, qr, gates, kc, vc, rows_new, wcache, win_new, pool, table, consts, nb, t_new, npages):
    fold, foldt, cover, blk1h = consts
    gw = N_KV_B * HEAD_DIM
    n_c = kc.shape[1]
    nw = wcache.shape[1]
    tok = lambda w: pl.BlockSpec((None, t_new, w), lambda b, pt: (b, 0, 0))
    const = lambda a: pl.BlockSpec(a.shape, lambda b, pt: (0,) * a.ndim)

    def page_spec(k):
        return pl.BlockSpec((None, PAGE, 2 * gw), lambda b, pt: (pt[b * npages + k], 0, 1))

    grid_spec = pltpu.PrefetchScalarGridSpec(
        num_scalar_prefetch=1,
        grid=(nb,),
        in_specs=[tok(D_MODEL), tok(D_MODEL), tok(LANES),
                  pl.BlockSpec((None, n_c, gw), lambda b, pt: (b, 0, 0)),
                  pl.BlockSpec((None, n_c, gw), lambda b, pt: (b, 0, 0)),
                  pl.BlockSpec((None, t_new, 2 * gw), lambda b, pt: (b, 0, 1)),
                  pl.BlockSpec((None, nw, 2 * gw), lambda b, pt: (b, 0, 0)),
                  tok(2 * gw), const(fold), const(foldt), const(cover), const(blk1h)]
                 + [page_spec(k) for k in range(npages)],
        out_specs=pl.BlockSpec((None, t_new, D_MODEL), lambda b, pt: (b, 0, 0)),
    )
    kern = functools.partial(_nsa_sample_kernel, npages=npages, t_new=t_new,
                             n_c_valid=(npages * PAGE + t_new - CMP_LEN) // CMP_STRIDE + 1,
                             n_s_valid=-(-(npages * PAGE + t_new) // SLC_LEN))
    out = pl.pallas_call(
        kern, grid_spec=grid_spec,
        out_shape=jax.ShapeDtypeStruct((nb, t_new, D_MODEL), F32),
        compiler_params=_cparams(("arbitrary",)), name="nsa_sample",
    )(table, q.reshape(nb, t_new, D_MODEL), qr.reshape(nb, t_new, D_MODEL), gates.reshape(nb, t_new, LANES),
      kc, vc, rows_new.reshape(nb, t_new, 4 * gw), wcache, win_new.reshape(nb, t_new, 2 * gw),
      fold, foldt, cover, blk1h, *([pool] * npages))
    return out.reshape(nb * t_new, D_MODEL)


def _gate_expand(branch):
    r = jnp.arange(LANES)[:, None]
    c = jnp.arange(D_MODEL)[None, :]
    return (r == 3 * (c // HEAD_DIM) + branch).astype(BF16)


def _cover_matrix(n_c_rows, n_c_valid, n_s):
    c = jnp.arange(n_c_rows)[:, None]
    s = jnp.arange(LANES)[None, :]
    c_start = c * CMP_STRIDE
    s_start = s * SLC_LEN
    hit = (c_start < s_start + SLC_LEN) & (c_start + CMP_LEN > s_start) & (c < n_c_valid) & (s < n_s)
    return hit.astype(BF16)


def _fold_matrix():
    r = jnp.arange(D_MODEL)[:, None]
    c = jnp.arange(N_KV_B * HEAD_DIM)[None, :]
    return ((r // (HPG_B * HEAD_DIM) == c // HEAD_DIM) & (r % HEAD_DIM == c % HEAD_DIM)).astype(BF16)


def _prompt_mod_spec(tiles_per_batch):
    return pl.BlockSpec((None, 1, D_MODEL), lambda i, *_: (i // tiles_per_batch, 0, 0))


def _row_mod_spec(tm):
    return pl.BlockSpec((tm, D_MODEL), lambda i, *_: (i, 0))


def kernel(x_prompt, x_sample, c_prompt, c_sample, cache_a0, cache_a1, cache_a2, cache_b_pool, cache_b_win,
           page_table, g_norm, w_ada, b_ada, w_qkv_a, w_o_a, g_kv, w_ada_kv, b_ada_kv, w_kv_b, pe_cmp, w_cmp1,
           b_cmp1, w_cmp2, b_cmp2, w_qg_b, w_o_b, w_ffn_in, w_ffn_out, g_final):
    bsz, seq, d = x_prompt.shape
    nb, t_new, _ = x_sample.shape
    depth = w_ada.shape[0]
    n_a = w_qkv_a.shape[0]
    past = page_table.shape[1] * PAGE
    npages = page_table.shape[1]
    gw = N_KV_B * HEAD_DIM
    hq = N_HEADS * HEAD_DIM

    w_ada_b = w_ada.astype(BF16)
    w_ada_kv_b = w_ada_kv.astype(BF16)[None]
    w_qkv_b = w_qkv_a.astype(BF16)
    w_o_a_b = w_o_a.astype(BF16)
    w_kv_bb = w_kv_b.astype(BF16)
    w_q_b = w_qg_b[:, :, :hq].astype(BF16)
    w_g_b = jnp.pad(w_qg_b[:, :, hq:], ((0, 0), (0, 0), (0, LANES - 3 * N_HEADS))).astype(BF16)
    w_o_b_b = w_o_b.astype(BF16)
    w_in_b = w_ffn_in.astype(BF16)
    w_out_b = w_ffn_out.astype(BF16)
    cw = _compress_weights(pe_cmp, w_cmp1, b_cmp1, w_cmp2, b_cmp2)

    n_c_all = bsz + nb
    mp = -(-n_c_all // 8) * 8
    c_all = jnp.pad(jnp.concatenate([c_prompt, c_sample], axis=0), ((0, mp - n_c_all), (0, 0)))
    mod_all = _adaln(c_all, w_ada_b, b_ada[:, None, :], 6 * d // 4).reshape(depth, mp, 6, d)
    mkv_all = _adaln(c_all, w_ada_kv_b, b_ada_kv[None, None, :], d)[0].reshape(mp, 2, d)

    e_c, e_s, e_w = _gate_expand(0), _gate_expand(1), _gate_expand(2)
    dummy_tab = jnp.zeros((8, LANES), F32)
    dummy_spec = pl.BlockSpec((8, LANES), lambda i, j: (0, 0))

    def run_trunk(x, mods, mkv, mod_spec_of, tabs, tab_spec_of, tm, mix_a, make_shared, mix_b):
        cos, sin = tabs
        a_states = []
        shared = None
        shared_state = None
        for layer in range(depth):
            md = mods[layer]
            ms = mod_spec_of(tm)
            if layer == n_a:
                rows = _proj(x, mkv[0], mkv[1], ms, g_kv[None], w_kv_bb[:, :4 * gw], cos, sin, tab_spec_of(tm),
                             tm=tm, tn=256, mode="rule", rope_rule=(4, (2,)), name="nsa_kv_rows")
                winr = _proj(x, mkv[0], mkv[1], ms, g_kv[None], w_kv_bb[:, 4 * gw:], cos, sin, tab_spec_of(tm),
                             tm=tm, tn=256, mode="rule", rope_rule=(2, (0,)), name="nsa_kv_win")
                shared, shared_state = make_shared(rows, winr)
            gn = g_norm[layer, 0][None]
            if layer < n_a:
                qkv = _proj(x, md[0], md[1], ms, gn, w_qkv_b[layer], cos, sin, tab_spec_of(tm),
                            tm=tm, tn=512, mode="rule", rope_rule=(12, tuple(range(8))), name="qkv_a")
                acts, st = mix_a(layer, qkv)
                a_states.append(st)
                tmo = min(tm, 256)
                x = _oproj(acts, w_o_a_b[layer], x, md[2], mod_spec_of(tmo), tm=tmo, n_grp=3, name="oproj_a")
            else:
                lb = layer - n_a
                q, qr = _proj(x, md[0], md[1], ms, gn, w_q_b[lb], cos, sin, tab_spec_of(tm),
                              tm=tm, tn=512, mode="both", name="q_b")
                gates = _proj(x, md[0], md[1], ms, gn, w_g_b[lb], dummy_tab, dummy_tab, dummy_spec,
                              tm=tm, tn=LANES, mode="sigmoid", name="gates_b")
                o = mix_b(q, qr, gates, shared)
                tmo = min(tm, 512)
                x = _oproj([o], w_o_b_b[lb], x, md[2], mod_spec_of(tmo), tm=tmo, n_grp=0, name="oproj_b")
            x = _ffn(x, md[3], md[4], md[5], ms, g_norm[layer, 1][None], w_in_b[layer], w_out_b[layer],
                     tm=tm, tf=256, name="ffn")
        y = _final_norm(x, g_final[None], min(tm, 512))
        return y, a_states, shared_state

    tm_p = 1024
    mods_p = [[mod_all[l, :bsz, k][:, None, :] for k in range(6)] for l in range(depth)]
    mkv_p = [mkv_all[:bsz, k][:, None, :] for k in range(2)]
    tabs_p = _rope_tables(jnp.arange(seq))

    def mod_spec_p(tm):
        return _prompt_mod_spec(seq // tm)

    def tab_spec_p(tm):
        tpb = seq // tm
        return pl.BlockSpec((tm, LANES), lambda i, j: (i % tpb, 0))

    def mix_a_p(layer, qkv):
        q3 = qkv.reshape(bsz, seq, -1)
        outs, lses = [], []
        for grp in range(len(A_PATTERNS)):
            o, l = _dil_prompt(q3, grp, bsz, seq)
            outs.append(o)
            lses.append(l)
        st = tuple(q3[:, seq - min(win, seq):, (3 * g + 1) * d:(3 * g + 3) * d]
                   .reshape(bsz, min(win, seq), 2, N_HEADS, HEAD_DIM) for g, (win, _) in enumerate(A_PATTERNS))
        return outs + lses, st

    n_c_p = (seq - CMP_LEN) // CMP_STRIDE + 1
    n_s_p = seq // SLC_LEN
    cover_p = _cover_matrix(seq // CMP_STRIDE, n_c_p, n_s_p)

    def shared_p(rows, winr):
        table = jnp.arange(bsz * (seq // PAGE), dtype=jnp.int32)
        kc, vc = _compress(rows.reshape(bsz * (seq // PAGE), PAGE, 4 * gw), table, bsz, seq // PAGE, cw)
        r5 = rows.reshape(bsz, seq, 4, N_KV_B, HEAD_DIM)
        blk1h = (jnp.arange(seq)[:, None] // SLC_LEN == jnp.arange(n_s_p)[None, :]).astype(BF16)
        ks = r5[:, :, 2].transpose(0, 2, 1, 3).astype(BF16)
        ka = jnp.concatenate([ks, jnp.broadcast_to(blk1h[None, None], (bsz, N_KV_B, seq, n_s_p))], axis=-1)
        vs = r5[:, :, 3].transpose(0, 2, 1, 3).astype(BF16)
        w4 = winr.reshape(bsz, seq, 2, N_KV_B, HEAD_DIM)
        wpad = jnp.pad(w4, ((0, 0), (WIN_B, 0), (0, 0), (0, 0), (0, 0)))
        kw = wpad[:, :, 0].transpose(0, 2, 1, 3).astype(BF16)
        vw = wpad[:, :, 1].transpose(0, 2, 1, 3).astype(BF16)
        state = (r5, w4[:, seq - min(WIN_B, seq):])
        return (kc, vc, ka, vs, kw, vw), state

    def mix_b_p(q, qr, gates, shared):
        kc, vc, ka, vs, kw, vw = shared
        oc, sb = _cmp_prompt(q, gates, kc, vc, cover_p, e_c, bsz, seq)
        o = _selwin_prompt(qr, sb, gates, oc, ka, vs, kw, vw, e_s, e_w, bsz, seq)
        return o.reshape(bsz * seq, d)

    y_p, a_p, (rows_p, win_p) = run_trunk(x_prompt.reshape(bsz * seq, d), mods_p, mkv_p, mod_spec_p, tabs_p,
                                          tab_spec_p, tm_p, mix_a_p, shared_p, mix_b_p)

    m_s = nb * t_new
    tm_s = min(m_s, 512)
    mods_s = [[jnp.repeat(mod_all[l, bsz:bsz + nb, k], t_new, axis=0) for k in range(6)] for l in range(depth)]
    mkv_s = [jnp.repeat(mkv_all[bsz:bsz + nb, k], t_new, axis=0) for k in range(2)]
    cos_s, sin_s = _rope_tables(past + jnp.arange(t_new))
    tabs_s = (jnp.tile(cos_s, (nb, 1)), jnp.tile(sin_s, (nb, 1)))
    caches = [c.reshape(c.shape[0], nb, c.shape[2], 2 * d) for c in (cache_a0, cache_a1, cache_a2)]

    def tab_spec_s(tm):
        return pl.BlockSpec((tm, LANES), lambda i, j: (i, 0))

    def mix_a_s(layer, qkv):
        outs, lses = [], []
        for grp in range(len(A_PATTERNS)):
            o, l = _dil_sample(qkv, caches[grp], layer, grp, nb, t_new)
            outs.append(o)
            lses.append(l)
        q3 = qkv.reshape(nb, t_new, -1)
        st = tuple(q3[:, :, (3 * g + 1) * d:(3 * g + 3) * d].reshape(nb, t_new, 2, N_HEADS, HEAD_DIM)
                   for g in range(len(A_PATTERNS)))
        return outs + lses, st

    n_c_s = (past + t_new - CMP_LEN) // CMP_STRIDE + 1
    n_s_s = -(-(past + t_new) // SLC_LEN)
    cover_s = _cover_matrix(past // CMP_STRIDE, n_c_s, n_s_s)
    fold = _fold_matrix()
    blk1h_s = (jnp.arange(LANES)[:, None] == jnp.arange(past + LANES)[None, :] // SLC_LEN).astype(BF16)
    table_s = page_table.reshape(-1).astype(jnp.int32)
    pool3 = cache_b_pool.reshape(cache_b_pool.shape[0], PAGE, 4 * gw)
    wcache = cache_b_win.reshape(nb, cache_b_win.shape[1], 2 * gw)

    def shared_s(rows, winr):
        kc, vc = _compress(pool3, table_s, nb, npages, cw)
        w4 = winr.reshape(nb, t_new, 2, N_KV_B, HEAD_DIM)
        win_full = jnp.concatenate([cache_b_win, w4], axis=1)
        n_keep = min(WIN_B, win_full.shape[1])
        state = (rows.reshape(nb, t_new, 4, N_KV_B, HEAD_DIM), win_full[:, win_full.shape[1] - n_keep:])
        return (kc, vc, rows, winr), state

    def mix_b_s(q, qr, gates, shared):
        kc, vc, rows, winr = shared
        return _nsa_sample(q, qr, gates, kc, vc, rows, wcache, winr, pool3, table_s,
                           (fold, fold.T, cover_s, blk1h_s), nb, t_new, npages)

    y_s, a_s, (rows_s, win_s) = run_trunk(x_sample.reshape(m_s, d), mods_s, mkv_s, _row_mod_spec, tabs_s,
                                          tab_spec_s, tm_s, mix_a_s, shared_s, mix_b_s)

    outs = [y_p.reshape(bsz, seq, d), y_s.reshape(nb, t_new, d)]
    for g, cache in enumerate((cache_a0, cache_a1, cache_a2)):
        outs.append(jnp.stack([st[g] for st in a_p]))
        new = jnp.stack([st[g] for st in a_s])
        full = jnp.concatenate([cache, new], axis=2)
        n_keep = min(A_PATTERNS[g][0], full.shape[2])
        outs.append(full[:, :, full.shape[2] - n_keep:])
    outs += [rows_p, rows_s, win_p, win_s]
    return tuple(outs)
```

```python
import functools

import jax
import jax.numpy as jnp
from jax import lax
from jax.experimental import pallas as pl
from jax.experimental.pallas import tpu as pltpu

F32 = jnp.float32
BF16 = jnp.bfloat16

D_MODEL = 1024
HEAD_DIM = 64
N_HEADS = 16
A_PATTERNS = ((128, 1), (512, 4), (2048, 16))
N_KV_B = 4
HPG_B = 4
CMP_LEN = 32
CMP_STRIDE = 16
SLC_LEN = 64
N_SELECT = 16
WIN_B = 512
PAGE = 128
FORCE_SCORE = 1.0e4
ROPE_THETA = 10000.0
EPS = 1e-6
NEG = -1e30
Q_SCALE = HEAD_DIM ** -0.5
LANES = 128


def _cparams(sem, vmem_mb=48):
    return pltpu.CompilerParams(dimension_semantics=sem, vmem_limit_bytes=vmem_mb << 20)


def _dot(a, b):
    return jnp.dot(a, b, preferred_element_type=F32)


def _dot_nt(a, b):
    return lax.dot_general(a, b, (((1,), (1,)), ((), ())), preferred_element_type=F32)


def _split3(x):
    hi = x.astype(BF16)
    r1 = x - hi.astype(F32)
    mid = r1.astype(BF16)
    lo = (r1 - mid.astype(F32)).astype(BF16)
    return hi, mid, lo


def _dot3(x, m01):
    hi, mid, lo = _split3(x)
    return _dot(hi, m01) + _dot(mid, m01) + _dot(lo, m01)


def _norm_mod(x, g, sh, sc):
    ms = jnp.mean(x * x, axis=-1, keepdims=True)
    y = x * lax.rsqrt(ms + EPS) * g
    return y * (1.0 + sc) + sh


def _rope_tile(a, cos, sin):
    tn = a.shape[1]
    reps = tn // LANES
    if reps > 1:
        cos = jnp.concatenate([cos] * reps, axis=1)
        sin = jnp.concatenate([sin] * reps, axis=1)
    lane = lax.broadcasted_iota(jnp.int32, a.shape, 1)
    first = (lane % HEAD_DIM) < (HEAD_DIM // 2)
    rot = jnp.where(first, pltpu.roll(a, tn - HEAD_DIM // 2, 1), pltpu.roll(a, HEAD_DIM // 2, 1))
    return a * cos + rot * sin


def _rope_tables(pos):
    half = HEAD_DIM // 2
    inv_freq = ROPE_THETA ** (-jnp.arange(half, dtype=F32) / half)
    ang = pos.astype(F32)[:, None] * inv_freq[None, :]
    c, s = jnp.cos(ang), jnp.sin(ang)
    cos = jnp.tile(jnp.concatenate([c, c], axis=1), (1, LANES // HEAD_DIM))
    sin = jnp.tile(jnp.concatenate([-s, s], axis=1), (1, LANES // HEAD_DIM))
    return cos, sin


def _ada_kernel(c_ref, w_ref, b_ref, o_ref):
    c = c_ref[...]
    s = (c * jax.nn.sigmoid(c)).astype(BF16)
    o_ref[...] = _dot(s, w_ref[...]) + b_ref[...]


def _adaln(c, w, b, tn):
    mp, d = c.shape
    n_l, _, n = w.shape
    return pl.pallas_call(
        _ada_kernel,
        grid=(n_l, n // tn),
        in_specs=[pl.BlockSpec((mp, d), lambda l, j: (0, 0)),
                  pl.BlockSpec((None, d, tn), lambda l, j: (l, 0, j)),
                  pl.BlockSpec((None, 1, tn), lambda l, j: (l, 0, j))],
        out_specs=pl.BlockSpec((None, mp, tn), lambda l, j: (l, 0, j)),
        out_shape=jax.ShapeDtypeStruct((n_l, mp, n), F32),
        compiler_params=_cparams(("arbitrary", "arbitrary")),
        name="adaln",
    )(c, w, b)


def _proj_kernel(x_ref, sh_ref, sc_ref, g_ref, w_ref, cos_ref, sin_ref, *rest, mode, tn, rope_rule):
    n_out = 2 if mode == "both" else 1
    out_refs = rest[:n_out]
    h_scr = rest[n_out]
    j = pl.program_id(1)

    @pl.when(j == 0)
    def _():
        h_scr[...] = _norm_mod(x_ref[...], g_ref[...], sh_ref[...], sc_ref[...]).astype(BF16)

    acc = _dot(h_scr[...], w_ref[...])
    if mode == "sigmoid":
        out_refs[0][...] = jax.nn.sigmoid(acc)
    elif mode == "both":
        out_refs[0][...] = acc
        out_refs[1][...] = _rope_tile(acc, cos_ref[...], sin_ref[...])
    else:
        period, units = rope_rule
        unit = ((j * tn) // 256) % period
        flag = unit == units[0]
        for u in units[1:]:
            flag = jnp.logical_or(flag, unit == u)

        @pl.when(flag)
        def _():
            out_refs[0][...] = _rope_tile(acc, cos_ref[...], sin_ref[...])

        @pl.when(jnp.logical_not(flag))
        def _():
            out_refs[0][...] = acc


def _proj(x, sh, sc, mod_spec, g, w, cos, sin, tab_spec, *, tm, tn, mode, rope_rule=None, name):
    m, k = x.shape
    n = w.shape[1]
    n_out = 2 if mode == "both" else 1
    kern = functools.partial(_proj_kernel, mode=mode, tn=tn, rope_rule=rope_rule)
    out_spec = pl.BlockSpec((tm, tn), lambda i, j: (i, j))
    out = pl.pallas_call(
        kern,
        grid=(m // tm, n // tn),
        in_specs=[pl.BlockSpec((tm, k), lambda i, j: (i, 0)), mod_spec, mod_spec,
                  pl.BlockSpec((1, k), lambda i, j: (0, 0)),
                  pl.BlockSpec((k, tn), lambda i, j: (0, j)), tab_spec, tab_spec],
        out_specs=[out_spec] * n_out,
        out_shape=[jax.ShapeDtypeStruct((m, n), F32)] * n_out,
        scratch_shapes=[pltpu.VMEM((tm, k), BF16)],
        compiler_params=_cparams(("arbitrary", "arbitrary")),
        name=name,
    )(x, sh, sc, g, w, cos, sin)
    return out if n_out == 2 else out[0]


def _oproj_kernel(*refs, n_grp):
    if n_grp:
        o_refs, l_refs = refs[:n_grp], refs[n_grp:2 * n_grp]
        ls = [r[...] for r in l_refs]
        mx = ls[0]
        for l in ls[1:]:
            mx = jnp.maximum(mx, l)
        es = [jnp.exp(l - mx) for l in ls]
        den = es[0]
        num = es[0] * o_refs[0][...]
        for e, o in zip(es[1:], o_refs[1:]):
            den = den + e
            num = num + e * o[...]
        a = num / den
        rest = refs[2 * n_grp:]
    else:
        a = refs[0][...]
        rest = refs[1:]
    w_ref, x_ref, gt_ref, out_ref = rest
    out_ref[...] = x_ref[...] + gt_ref[...] * _dot(a.astype(BF16), w_ref[...])


def _oproj(acts, w, x, gate, mod_spec, *, tm, n_grp, name):
    m, d = x.shape
    k = w.shape[0]
    row = pl.BlockSpec((tm, k), lambda i: (i, 0))
    return pl.pallas_call(
        functools.partial(_oproj_kernel, n_grp=n_grp),
        grid=(m // tm,),
        in_specs=[row] * len(acts) + [pl.BlockSpec((k, d), lambda i: (0, 0)),
                                      pl.BlockSpec((tm, d), lambda i: (i, 0)), mod_spec],
        out_specs=pl.BlockSpec((tm, d), lambda i: (i, 0)),
        out_shape=jax.ShapeDtypeStruct((m, d), F32),
        compiler_params=_cparams(("arbitrary",)),
        name=name,
    )(*acts, w, x, gate)


def _ffn_kernel(x_ref, sh_ref, sc_ref, gt_ref, g_ref, wa_ref, wg_ref, wo_ref, out_ref, h_scr, acc_scr):
    f = pl.program_id(1)

    @pl.when(f == 0)
    def _():
        h_scr[...] = _norm_mod(x_ref[...], g_ref[...], sh_ref[...], sc_ref[...]).astype(BF16)
        acc_scr[...] = jnp.zeros_like(acc_scr)

    h = h_scr[...]
    a = _dot(h, wa_ref[...])
    gg = _dot(h, wg_ref[...])
    act = (a * jax.nn.sigmoid(a)) * gg
    acc_scr[...] += _dot(act.astype(BF16), wo_ref[...])

    @pl.when(f == pl.num_programs(1) - 1)
    def _():
        out_ref[...] = x_ref[...] + gt_ref[...] * acc_scr[...]


def _ffn(x, sh, sc, gt, mod_spec, g, w_in, w_out, *, tm, tf, name):
    m, d = x.shape
    dff = w_out.shape[0]
    nf = dff // tf
    return pl.pallas_call(
        _ffn_kernel,
        grid=(m // tm, nf),
        in_specs=[pl.BlockSpec((tm, d), lambda i, f: (i, 0)), mod_spec, mod_spec, mod_spec,
                  pl.BlockSpec((1, d), lambda i, f: (0, 0)),
                  pl.BlockSpec((d, tf), lambda i, f: (0, f)),
                  pl.BlockSpec((d, tf), lambda i, f: (0, f + nf)),
                  pl.BlockSpec((tf, d), lambda i, f: (f, 0))],
        out_specs=pl.BlockSpec((tm, d), lambda i, f: (i, 0)),
        out_shape=jax.ShapeDtypeStruct((m, d), F32),
        scratch_shapes=[pltpu.VMEM((tm, d), BF16), pltpu.VMEM((tm, d), F32)],
        compiler_params=_cparams(("arbitrary", "arbitrary")),
        name=name,
    )(x, sh, sc, gt, g, w_in, w_in, w_out)


def _final_norm_kernel(x_ref, g_ref, o_ref):
    x = x_ref[...]
    ms = jnp.mean(x * x, axis=-1, keepdims=True)
    o_ref[...] = x * lax.rsqrt(ms + EPS) * g_ref[...]


def _final_norm(x, g, tm):
    m, d = x.shape
    return pl.pallas_call(
        _final_norm_kernel,
        grid=(m // tm,),
        in_specs=[pl.BlockSpec((tm, d), lambda i: (i, 0)), pl.BlockSpec((1, d), lambda i: (0, 0))],
        out_specs=pl.BlockSpec((tm, d), lambda i: (i, 0)),
        out_shape=jax.ShapeDtypeStruct((m, d), F32),
        compiler_params=_cparams(("arbitrary",)),
        name="final_norm",
    )(x, g)


A_BLK = 128


def _dil_prompt_kernel(q_ref, kp_ref, kc_ref, vp_ref, vc_ref, o_ref, l_ref, s_scr, p_scr, den_scr):
    i = pl.program_id(2)
    row = lax.broadcasted_iota(jnp.int32, (A_BLK, A_BLK), 0)
    col = lax.broadcasted_iota(jnp.int32, (A_BLK, A_BLK), 1)
    m_prev = jnp.logical_and(col >= row, i > 0)
    m_cur = col <= row
    heads = [slice(h * HEAD_DIM, (h + 1) * HEAD_DIM) for h in range(N_HEADS)]
    for h, sl in enumerate(heads):
        q = (q_ref[:, sl] * Q_SCALE).astype(BF16)
        s_scr[h, :, :A_BLK] = _dot_nt(q, kp_ref[:, sl].astype(BF16))
        s_scr[h, :, A_BLK:] = _dot_nt(q, kc_ref[:, sl].astype(BF16))
    for h, sl in enumerate(heads):
        s_p = jnp.where(m_prev, s_scr[h, :, :A_BLK], NEG)
        s_c = jnp.where(m_cur, s_scr[h, :, A_BLK:], NEG)
        mx = jnp.max(jnp.maximum(s_p, s_c), axis=-1, keepdims=True)
        p_p = jnp.exp(s_p - mx)
        p_c = jnp.exp(s_c - mx)
        den = jnp.sum(p_p + p_c, axis=-1, keepdims=True)
        p_scr[h, :, :A_BLK] = p_p.astype(BF16)
        p_scr[h, :, A_BLK:] = p_c.astype(BF16)
        den_scr[h] = jnp.broadcast_to(den, (A_BLK, HEAD_DIM))
        l_ref[:, sl] = jnp.broadcast_to(mx + jnp.log(den), (A_BLK, HEAD_DIM))
    for h, sl in enumerate(heads):
        o = (_dot(p_scr[h, :, :A_BLK], vp_ref[:, sl].astype(BF16))
             + _dot(p_scr[h, :, A_BLK:], vc_ref[:, sl].astype(BF16)))
        o_ref[:, sl] = o / den_scr[h]


def _dil_prompt(qkv, grp, bsz, t):
    _, dil = A_PATTERNS[grp]
    tu = t // dil
    ncol = qkv.shape[-1] // D_MODEL
    qv = qkv.reshape(bsz, tu, dil * qkv.shape[-1])

    def spec(which, prev):
        if prev:
            return pl.BlockSpec((None, A_BLK, D_MODEL), lambda b, r, i: (b, jnp.maximum(i - 1, 0), r * ncol + which))
        return pl.BlockSpec((None, A_BLK, D_MODEL), lambda b, r, i: (b, i, r * ncol + which))

    out_spec = pl.BlockSpec((None, A_BLK, D_MODEL), lambda b, r, i: (b, i, r))
    shp = jax.ShapeDtypeStruct((bsz, tu, dil * D_MODEL), F32)
    o, l = pl.pallas_call(
        _dil_prompt_kernel,
        grid=(bsz, dil, tu // A_BLK),
        in_specs=[spec(0, False), spec(1, True), spec(1, False), spec(2, True), spec(2, False)],
        out_specs=[out_spec, out_spec],
        out_shape=[shp, shp],
        scratch_shapes=[pltpu.VMEM((N_HEADS, A_BLK, 2 * A_BLK), F32), pltpu.VMEM((N_HEADS, A_BLK, 2 * A_BLK), BF16),
                        pltpu.VMEM((N_HEADS, A_BLK, HEAD_DIM), F32)],
        compiler_params=_cparams(("arbitrary", "arbitrary", "arbitrary")),
        name=f"dil_prompt_g{grp}",
    )(qv, qv, qv, qv, qv)
    return o.reshape(bsz * t, D_MODEL), l.reshape(bsz * t, D_MODEL)


def _head_mask(rows, cols, row_head, col_div):
    r = lax.broadcasted_iota(jnp.int32, (rows, cols), 0)
    c = lax.broadcasted_iota(jnp.int32, (rows, cols), 1)
    return row_head(r) == c // col_div


def _dil_sample_kernel(q_ref, kn_ref, vn_ref, cur_ref, nxt_ref, *rest, dil, wc, t_new, aliased):
    if aliased:
        rest = rest[1:]
    o_ref, l_ref, st_ref, qx_scr, new_scr, m_scr, l_scr, acc_scr = rest
    c = pl.program_id(1)
    last = c == pl.num_programs(1) - 1
    nrow = N_HEADS * t_new
    hmask = _head_mask(nrow, D_MODEL, lambda r: r // t_new, HEAD_DIM)

    @pl.when(c == 0)
    def _():
        q = q_ref[...] * Q_SCALE
        qt = jnp.broadcast_to(q[None], (N_HEADS, t_new, D_MODEL)).reshape(nrow, D_MODEL)
        qx_scr[...] = jnp.where(hmask, qt, 0.0).astype(BF16)
        pad = jnp.zeros((LANES - t_new, D_MODEL), F32)
        new_scr[:D_MODEL, :] = jnp.concatenate([kn_ref[...], pad], axis=0).T
        new_scr[D_MODEL:, :] = jnp.concatenate([vn_ref[...], pad], axis=0).T
        m_scr[...] = jnp.full_like(m_scr, NEG)
        l_scr[...] = jnp.zeros_like(l_scr)
        acc_scr[...] = jnp.zeros_like(acc_scr)

    qx = qx_scr[...]

    def update(s, v_t):
        m_old = m_scr[...]
        m_new = jnp.maximum(m_old, jnp.max(s, axis=-1, keepdims=True))
        alpha = jnp.exp(m_old - m_new)
        p = jnp.exp(s - m_new)
        l_scr[...] = alpha * l_scr[...] + jnp.sum(p, axis=-1, keepdims=True)
        acc_scr[...] = alpha * acc_scr[...] + _dot_nt(p.astype(BF16), v_t)
        m_scr[...] = m_new

    cur = cur_ref[...]
    qi = lax.broadcasted_iota(jnp.int32, (nrow, wc), 0) % t_new
    rho = c * wc + lax.broadcasted_iota(jnp.int32, (nrow, wc), 1)
    valid = jnp.logical_and((rho - qi) % dil == 0, rho >= qi)
    update(jnp.where(valid, _dot(qx, cur[:D_MODEL].astype(BF16)), NEG), cur[D_MODEL:].astype(BF16))

    rolled = pltpu.roll(cur, wc - t_new, 1)
    tail = pltpu.roll(jnp.where(last, new_scr[...], nxt_ref[...]), LANES - t_new, 1)
    lane = lax.broadcasted_iota(jnp.int32, (2 * D_MODEL, LANES), 1)
    if wc > LANES:
        st_ref[:, :wc - LANES] = rolled[:, :wc - LANES]
    st_ref[:, wc - LANES:] = jnp.where(lane < LANES - t_new, rolled[:, wc - LANES:], tail)

    @pl.when(last)
    def _():
        qi2 = lax.broadcasted_iota(jnp.int32, (nrow, LANES), 0) % t_new
        kj = lax.broadcasted_iota(jnp.int32, (nrow, LANES), 1)
        ok = jnp.logical_and(kj <= qi2, (qi2 - kj) % dil == 0)
        update(jnp.where(ok, _dot(qx, new_scr[:D_MODEL, :].astype(BF16)), NEG), new_scr[D_MODEL:, :].astype(BF16))
        den = l_scr[...]
        res = jnp.where(hmask, acc_scr[...] / den, 0.0)
        lse = jnp.where(hmask, m_scr[...] + jnp.log(den), 0.0)
        o_ref[...] = jnp.sum(res.reshape(N_HEADS, t_new, D_MODEL), axis=0)
        l_ref[...] = jnp.sum(lse.reshape(N_HEADS, t_new, D_MODEL), axis=0)


def _dil_sample(qkv, cache_t, state, layer, grp, nb, t_new):
    win, dil = A_PATTERNS[grp]
    wc = min(win, 512)
    n_chunks = win // wc
    qv = qkv.reshape(nb, t_new, qkv.shape[-1])
    nrow = N_HEADS * t_new
    lanes_per_chunk = wc // LANES

    def new_spec(which):
        return pl.BlockSpec((None, t_new, D_MODEL), lambda b, c: (b, 0, which))

    def nxt_map(b, c):
        return (layer, b, 0, jnp.minimum((c + 1) * lanes_per_chunk, win // LANES - 1))

    out_spec = pl.BlockSpec((None, t_new, D_MODEL), lambda b, c: (b, 0, 0))
    shp = jax.ShapeDtypeStruct((nb, t_new, D_MODEL), F32)
    in_specs = [new_spec(0), new_spec(1), new_spec(2),
                pl.BlockSpec((None, None, 2 * D_MODEL, wc), lambda b, c: (layer, b, 0, c)),
                pl.BlockSpec((None, None, 2 * D_MODEL, LANES), nxt_map)]
    args = [qv, qv, qv, cache_t, cache_t]
    aliases = {}
    if state is not None:
        in_specs.append(pl.BlockSpec(memory_space=pl.ANY))
        args.append(state)
        aliases = {len(args) - 1: 2}
    o, l, st = pl.pallas_call(
        functools.partial(_dil_sample_kernel, dil=dil, wc=wc, t_new=t_new, aliased=state is not None),
        grid=(nb, n_chunks),
        in_specs=in_specs,
        out_specs=[out_spec, out_spec,
                   pl.BlockSpec((None, None, 2 * D_MODEL, wc), lambda b, c: (layer, b, 0, c))],
        out_shape=[shp, shp, jax.ShapeDtypeStruct(cache_t.shape, F32)],
        scratch_shapes=[pltpu.VMEM((nrow, D_MODEL), BF16), pltpu.VMEM((2 * D_MODEL, LANES), F32),
                        pltpu.VMEM((nrow, 1), F32), pltpu.VMEM((nrow, 1), F32), pltpu.VMEM((nrow, D_MODEL), F32)],
        input_output_aliases=aliases,
        compiler_params=_cparams(("arbitrary", "arbitrary"), 56),
        name=f"dil_sample_g{grp}",
    )(*args)
    return o.reshape(nb * t_new, D_MODEL), l.reshape(nb * t_new, D_MODEL), st


CMP_SUB = PAGE // CMP_STRIDE


CMP_PAGES = 4
CMP_CHUNKS = 4


def _compress_kernel(*refs):
    n_pg = CMP_PAGES * CMP_CHUNKS
    page_refs = refs[1:1 + n_pg]
    pelo_ref, pehi_ref, w1lo_ref, w1hi_ref, b1_ref, w2_ref, b2_ref, kc_ref, vc_ref, zlo, zhi = refs[1 + n_pg:]
    k = pl.program_id(1)
    nsteps = pl.num_programs(1)
    nu = zlo.shape[1]
    for pp in range(CMP_PAGES):
        r0 = pl.multiple_of((k * CMP_PAGES + pp) * CMP_SUB, CMP_SUB)
        for m in range(CMP_CHUNKS):
            page_ref = page_refs[pp * CMP_CHUNKS + m]
            for p in range(CMP_STRIDE):
                rows = page_ref[pl.ds(p, CMP_SUB, stride=CMP_STRIDE), :]
                lanes = slice(p * LANES, (p + 1) * LANES)
                zlo[m, pl.ds(r0, CMP_SUB), lanes] = rows + pelo_ref[p:p + 1, m * LANES:(m + 1) * LANES]
                zhi[m, pl.ds(r0, CMP_SUB), lanes] = rows + pehi_ref[p:p + 1, m * LANES:(m + 1) * LANES]

    @pl.when(k == nsteps - 1)
    def _():
        for m in range(4):
            ty = m // 2
            first = _dot(zlo[m].astype(BF16), w1lo_ref[ty])
            second = _dot(zhi[m].astype(BF16), w1hi_ref[ty])
            hid = first + pltpu.roll(second, nu - 1, 0) + b1_ref[ty]
            out = _dot(jax.nn.gelu(hid).astype(BF16), w2_ref[ty]) + b2_ref[ty]
            dst = kc_ref if ty == 0 else vc_ref
            dst[:, (m % 2) * LANES:(m % 2 + 1) * LANES] = out


def _compress(pages, table, nb, npages, cw):
    pelo, pehi, w1lo, w1hi, b1, w2, b2 = cw
    nu = npages * CMP_SUB
    const2 = lambda shape: pl.BlockSpec(shape, lambda b, k, pt: (0,) * len(shape))

    def chunk_spec(pp, m):
        return pl.BlockSpec((None, PAGE, LANES), lambda b, k, pt: (pt[b * npages + k * CMP_PAGES + pp], 0, m))

    n_pg = CMP_PAGES * CMP_CHUNKS
    grid_spec = pltpu.PrefetchScalarGridSpec(
        num_scalar_prefetch=1,
        grid=(nb, npages // CMP_PAGES),
        in_specs=[chunk_spec(pp, m) for pp in range(CMP_PAGES) for m in range(CMP_CHUNKS)]
                 + [const2(pelo.shape), const2(pehi.shape), const2(w1lo.shape), const2(w1hi.shape),
                    const2(b1.shape), const2(w2.shape), const2(b2.shape)],
        out_specs=[pl.BlockSpec((None, nu, N_KV_B * HEAD_DIM), lambda b, k, pt: (b, 0, 0))] * 2,
        scratch_shapes=[pltpu.VMEM((4, nu, CMP_STRIDE * LANES), F32)] * 2,
    )
    shp = jax.ShapeDtypeStruct((nb, nu, N_KV_B * HEAD_DIM), F32)
    return pl.pallas_call(
        _compress_kernel, grid_spec=grid_spec, out_shape=[shp, shp],
        compiler_params=_cparams(("arbitrary", "arbitrary")), name="nsa_compress",
    )(table, *([pages] * n_pg), pelo, pehi, w1lo, w1hi, b1, w2, b2)


def _compress_weights(pe_cmp, w_cmp1, b_cmp1, w_cmp2, b_cmp2):
    eye2 = jnp.eye(2, dtype=F32)
    hid = w_cmp1.shape[-1]
    w1 = w_cmp1.reshape(2, CMP_LEN, HEAD_DIM, hid)

    def pair_w1(w):
        return jnp.einsum("pen,ab->paebn", w, eye2).reshape(CMP_STRIDE * LANES, 2 * hid)

    w1lo = jnp.stack([pair_w1(w1[ty, :CMP_STRIDE]) for ty in range(2)]).astype(BF16)
    w1hi = jnp.stack([pair_w1(w1[ty, CMP_STRIDE:]) for ty in range(2)]).astype(BF16)
    w2 = jnp.stack([jnp.einsum("ne,ab->anbe", w_cmp2[ty], eye2).reshape(2 * hid, 2 * HEAD_DIM)
                    for ty in range(2)]).astype(BF16)
    b1 = jnp.tile(b_cmp1, (1, 2))[:, None, :]
    b2 = jnp.tile(b_cmp2, (1, 2))[:, None, :]
    pelo = jnp.concatenate([jnp.tile(pe_cmp[ty, :CMP_STRIDE], (1, N_KV_B)) for ty in range(2)], axis=1)
    pehi = jnp.concatenate([jnp.tile(pe_cmp[ty, CMP_STRIDE:], (1, N_KV_B)) for ty in range(2)], axis=1)
    return pelo, pehi, w1lo, w1hi, b1, w2, b2


C_TQ = 128


def _rank_select(imp_t, n_blk):
    sidx = lax.broadcasted_iota(jnp.int32, imp_t.shape, 0)
    cnt = jnp.zeros(imp_t.shape, jnp.int32)
    for s in range(n_blk):
        row = imp_t[s:s + 1, :]
        ahead = jnp.logical_or(row > imp_t, jnp.logical_and(row == imp_t, sidx > s))
        cnt = cnt + ahead.astype(jnp.int32)
    return cnt < N_SELECT


def _cmp_prompt_kernel(q_ref, gates_ref, kc_ref, vc_ref, cover_ref, ec_ref, oc_ref, sb_ref):
    i = pl.program_id(1)
    n_c = kc_ref.shape[0]
    n_s = sb_ref.shape[-1]
    t = i * C_TQ + lax.broadcasted_iota(jnp.int32, (C_TQ, n_c), 0)
    c_idx = lax.broadcasted_iota(jnp.int32, (C_TQ, n_c), 1)
    valid = c_idx * CMP_STRIDE + (CMP_LEN - 1) <= t
    any_valid = (t[:, :1] >= CMP_LEN - 1).astype(F32)
    tq = i * C_TQ + lax.broadcasted_iota(jnp.int32, (C_TQ, LANES), 0)
    jb = lax.broadcasted_iota(jnp.int32, (C_TQ, LANES), 1)
    cur = tq // SLC_LEN
    forced = jnp.logical_or(jb == 0, jnp.logical_or(jb == cur, jb == cur - 1))
    cover = cover_ref[...]
    gates = gates_ref[...]
    gh = gates.astype(BF16)
    gl = (gates - gh.astype(F32)).astype(BF16)
    gfull = _dot(gh, ec_ref[...]) + _dot(gl, ec_ref[...])
    for g in range(N_KV_B):
        gs = slice(g * HEAD_DIM, (g + 1) * HEAD_DIM)
        kc = kc_ref[:, gs].astype(BF16)
        vc = vc_ref[:, gs].astype(BF16)
        psum = jnp.zeros((C_TQ, n_c), F32)
        for j in range(HPG_B):
            h = g * HPG_B + j
            hs = slice(h * HEAD_DIM, (h + 1) * HEAD_DIM)
            q = (q_ref[:, hs] * Q_SCALE).astype(BF16)
            s = jnp.where(valid, _dot_nt(q, kc), NEG)
            mx = jnp.max(s, axis=-1, keepdims=True)
            e = jnp.exp(s - mx)
            p = e / jnp.sum(e, axis=-1, keepdims=True) * any_valid
            psum = psum + p
            oc_ref[:, hs] = _dot(p.astype(BF16), vc) * gfull[:, hs]
        imp = _dot3(psum, cover)
        imp = jnp.where(forced, FORCE_SCORE, jnp.where(jb <= cur, imp, -1.0))
        sel_t = _rank_select(imp.T[:n_s, :], n_s)
        bias_t = jnp.where(sel_t, 0.0, NEG)
        bias_t = jnp.concatenate([bias_t, jnp.zeros((LANES - n_s, C_TQ), F32)], axis=0)
        sb_ref[g] = bias_t.T[:, :n_s].astype(BF16)


def _cmp_prompt(q, gates, kc, vc, cover, e_c, bsz, t):
    n_c = kc.shape[1]
    n_s = t // SLC_LEN
    qv = q.reshape(bsz, t, D_MODEL)
    gv = gates.reshape(bsz, t, LANES)
    oc, sb = pl.pallas_call(
        _cmp_prompt_kernel,
        grid=(bsz, t // C_TQ),
        in_specs=[pl.BlockSpec((None, C_TQ, D_MODEL), lambda b, i: (b, i, 0)),
                  pl.BlockSpec((None, C_TQ, LANES), lambda b, i: (b, i, 0)),
                  pl.BlockSpec((None, n_c, N_KV_B * HEAD_DIM), lambda b, i: (b, 0, 0)),
                  pl.BlockSpec((None, n_c, N_KV_B * HEAD_DIM), lambda b, i: (b, 0, 0)),
                  pl.BlockSpec(cover.shape, lambda b, i: (0, 0)),
                  pl.BlockSpec(e_c.shape, lambda b, i: (0, 0))],
        out_specs=[pl.BlockSpec((None, C_TQ, D_MODEL), lambda b, i: (b, i, 0)),
                   pl.BlockSpec((None, N_KV_B, C_TQ, n_s), lambda b, i: (b, 0, i, 0))],
        out_shape=[jax.ShapeDtypeStruct((bsz, t, D_MODEL), F32),
                   jax.ShapeDtypeStruct((bsz, N_KV_B, t, n_s), BF16)],
        compiler_params=_cparams(("arbitrary", "arbitrary")),
        name="nsa_cmp_prompt",
    )(qv, gv, kc, vc, cover, e_c)
    return oc, sb


S_TQ = 64
S_TK = 256
W_TK = 192


def _selwin_prompt_kernel(qr_ref, sb_ref, gates_ref, oc_ref, ka_ref, vs_ref, kw_ref, vw_ref, es_ref, ew_ref, o_ref,
                          s_scr):
    i = pl.program_id(1)
    t0 = i * S_TQ
    nrow = HPG_B * S_TQ
    qr = qr_ref[...] * Q_SCALE
    tpos = t0 + lax.broadcasted_iota(jnp.int32, (nrow, 1), 0) % S_TQ
    zeros = jnp.zeros((nrow, HEAD_DIM), BF16)
    q_aug, q_win = [], []
    for g in range(N_KV_B):
        heads = [qr[:, (g * HPG_B + j) * HEAD_DIM:(g * HPG_B + j + 1) * HEAD_DIM] for j in range(HPG_B)]
        qa = jnp.concatenate(heads, axis=0).astype(BF16)
        q_aug.append(jnp.concatenate([qa, jnp.concatenate([sb_ref[g]] * HPG_B, axis=0)], axis=1))
        q_win.append(jnp.concatenate([qa, zeros] if g % 2 == 0 else [zeros, qa], axis=1))

    def online(carry, s, v):
        m_old, acc = carry
        m_new = jnp.maximum(m_old, jnp.max(s, axis=-1, keepdims=True))
        p = jnp.exp(s - m_new)
        return m_new, jnp.exp(m_old - m_new) * acc + _dot(p.astype(BF16), v)

    groups = tuple(range(N_KV_B))
    init = tuple((jnp.full((nrow, 1), NEG, F32), jnp.zeros((nrow, 2 * HEAD_DIM), F32)) for _ in groups)

    def scores(kt, g):
        return _dot_nt(q_aug[g], ka_ref[g, pl.ds(pl.multiple_of(kt * S_TK, S_TK), S_TK), :])

    def sel_step(kt, carry, diagonal):
        k0 = pl.multiple_of(kt * S_TK, S_TK)
        out = []
        for g in groups:
            s = s_scr[g]
            if diagonal:
                kpos = k0 + lax.broadcasted_iota(jnp.int32, (nrow, S_TK), 1)
                s = jnp.where(kpos <= tpos, s, NEG)
            else:
                s_scr[g] = scores(kt + 1, g)
            out.append(online(carry[g], s, vs_ref[g, pl.ds(k0, S_TK), :]))
        return tuple(out)

    n_full = t0 // S_TK
    for g in groups:
        s_scr[g] = scores(0, g)
    sel = lax.fori_loop(0, n_full, lambda kt, c: sel_step(kt, c, False), init)
    sel = sel_step(n_full, sel, True)

    n_win = (WIN_B + S_TQ) // W_TK
    win = []
    for g in groups:
        tiles = []
        for w in range(n_win):
            k0 = pl.multiple_of(t0 + w * W_TK, SLC_LEN)
            kpos = k0 - WIN_B + lax.broadcasted_iota(jnp.int32, (nrow, W_TK), 1)
            ok = jnp.logical_and(kpos >= 0, jnp.logical_and(kpos <= tpos, tpos - kpos <= WIN_B))
            tiles.append(jnp.where(ok, _dot_nt(q_win[g], kw_ref[g // 2, pl.ds(k0, W_TK), :]), NEG))
        mx = jnp.max(tiles[0], axis=-1, keepdims=True)
        for s in tiles[1:]:
            mx = jnp.maximum(mx, jnp.max(s, axis=-1, keepdims=True))
        acc = jnp.zeros((nrow, 2 * HEAD_DIM), F32)
        for w, s in enumerate(tiles):
            k0 = pl.multiple_of(t0 + w * W_TK, SLC_LEN)
            acc = acc + _dot(jnp.exp(s - mx).astype(BF16), vw_ref[g, pl.ds(k0, W_TK), :])
        win.append((mx, acc))

    gates = gates_ref[...]
    gh = gates.astype(BF16)
    gl = (gates - gh.astype(F32)).astype(BF16)
    g_s = _dot(gh, es_ref[...]) + _dot(gl, es_ref[...])
    g_w = _dot(gh, ew_ref[...]) + _dot(gl, ew_ref[...])
    for g in groups:
        o_s = sel[g][1][:, :HEAD_DIM] / sel[g][1][:, HEAD_DIM:HEAD_DIM + 1]
        o_w = win[g][1][:, :HEAD_DIM] / win[g][1][:, HEAD_DIM:HEAD_DIM + 1]
        for j in range(HPG_B):
            hs = slice((g * HPG_B + j) * HEAD_DIM, (g * HPG_B + j + 1) * HEAD_DIM)
            rs = slice(j * S_TQ, (j + 1) * S_TQ)
            o_ref[:, hs] = oc_ref[:, hs] + g_s[:, hs] * o_s[rs, :] + g_w[:, hs] * o_w[rs, :]


def _selwin_prompt(qr, sb, gates, oc, ka, vs, kw, vw, e_s, e_w, bsz, t):
    qv = qr.reshape(bsz, t, D_MODEL)
    gv = gates.reshape(bsz, t, LANES)
    n_s = sb.shape[-1]
    tp = kw.shape[2]
    tok = pl.BlockSpec((None, S_TQ, D_MODEL), lambda b, i: (b, i, 0))
    resident = lambda a: pl.BlockSpec((None,) + a.shape[1:], lambda b, i: (b, 0, 0, 0))
    return pl.pallas_call(
        _selwin_prompt_kernel,
        grid=(bsz, t // S_TQ),
        in_specs=[tok,
                  pl.BlockSpec((None, N_KV_B, S_TQ, n_s), lambda b, i: (b, 0, i, 0)),
                  pl.BlockSpec((None, S_TQ, LANES), lambda b, i: (b, i, 0)),
                  tok, resident(ka), resident(vs), resident(kw), resident(vw),
                  pl.BlockSpec(e_s.shape, lambda b, i: (0, 0)),
                  pl.BlockSpec(e_w.shape, lambda b, i: (0, 0))],
        out_specs=tok,
        out_shape=jax.ShapeDtypeStruct((bsz, t, D_MODEL), F32),
        scratch_shapes=[pltpu.VMEM((N_KV_B, HPG_B * S_TQ, S_TK), F32)],
        compiler_params=_cparams(("arbitrary", "arbitrary"), 56),
        name="nsa_selwin_prompt",
    )(qv, sb, gv, oc, ka, vs, kw, vw, e_s, e_w)


def _nsa_sample_kernel(pt_ref, q_ref, qr_ref, gates_ref, kc_ref, vc_ref, newrows_ref, wcache_ref, wnew_ref,
                       fold_ref, foldt_ref, cover_ref, blk1h_ref, *rest, npages, t_new, n_c_valid, n_s_valid):
    del pt_ref
    page_refs = rest[:npages]
    o_ref = rest[npages]
    nrow = N_HEADS * t_new
    gw = N_KV_B * HEAD_DIM
    row_head = lambda r: ((r // t_new) % N_KV_B) * HPG_B + r // (t_new * N_KV_B)
    hmask = _head_mask(nrow, D_MODEL, row_head, HEAD_DIM)
    fold = fold_ref[...]

    def qexp(ref):
        q = ref[...] * Q_SCALE
        qt = jnp.broadcast_to(q[None], (N_HEADS, t_new, D_MODEL)).reshape(nrow, D_MODEL)
        return _dot(jnp.where(hmask, qt, 0.0).astype(BF16), fold).astype(BF16)

    qc = qexp(q_ref)
    qr = qexp(qr_ref)
    qi = lax.broadcasted_iota(jnp.int32, (nrow, LANES), 0) % t_new
    lane = lax.broadcasted_iota(jnp.int32, (nrow, LANES), 1)
    new_ok = lane <= qi

    def pad_new(x):
        return jnp.concatenate([x, jnp.zeros((LANES - t_new, x.shape[1]), F32)], axis=0).astype(BF16)

    kc = kc_ref[...].astype(BF16)
    s = jnp.where(lane < n_c_valid, _dot_nt(qc, kc), NEG)
    e = jnp.exp(s - jnp.max(s, axis=-1, keepdims=True))
    p_c = e / jnp.sum(e, axis=-1, keepdims=True)
    res_c = _dot(p_c.astype(BF16), vc_ref[...].astype(BF16))

    ng = N_KV_B * t_new
    psum = p_c[0:ng] + p_c[ng:2 * ng] + p_c[2 * ng:3 * ng] + p_c[3 * ng:4 * ng]
    imp = _dot3(psum, cover_ref[...])
    jb = lax.broadcasted_iota(jnp.int32, (ng, LANES), 1)
    cur = n_s_valid - 1
    forced = jnp.logical_or(jb == 0, jnp.logical_or(jb == cur, jb == cur - 1))
    imp = jnp.where(forced, FORCE_SCORE, jnp.where(jb <= cur, imp, -2.0))
    cnt = jnp.zeros((ng, LANES), jnp.int32)
    for sblk in range(n_s_valid):
        colv = jnp.broadcast_to(imp[:, sblk:sblk + 1], (ng, LANES))
        ahead = jnp.logical_or(colv > imp, jnp.logical_and(colv == imp, jb > sblk))
        cnt = cnt + ahead.astype(jnp.int32)
    bias_g = jnp.where(cnt < N_SELECT, 0.0, NEG).astype(BF16)
    bias = jnp.concatenate([bias_g] * HPG_B, axis=0)
    key_bias = _dot(bias, blk1h_ref[...])

    past = npages * PAGE
    s_parts = [_dot_nt(qr, page_refs[k][:, :gw].astype(BF16)) for k in range(npages)]
    s_new = jnp.where(new_ok, _dot_nt(qr, pad_new(newrows_ref[:, :gw])), NEG)
    s_all = jnp.concatenate(s_parts + [s_new], axis=1) + key_bias
    mx = jnp.max(s_all, axis=-1, keepdims=True)
    p_s = jnp.exp(s_all - mx)
    den_s = jnp.sum(p_s, axis=-1, keepdims=True)
    p_sb = p_s.astype(BF16)
    res_s = _dot(p_sb[:, past:], pad_new(newrows_ref[:, gw:]))
    for k in range(npages):
        res_s = res_s + _dot(p_sb[:, k * PAGE:(k + 1) * PAGE], page_refs[k][:, gw:].astype(BF16))
    res_s = res_s / den_s

    nw = wcache_ref.shape[0]
    qiw = lax.broadcasted_iota(jnp.int32, (nrow, nw), 0) % t_new
    rw = lax.broadcasted_iota(jnp.int32, (nrow, nw), 1)
    s_w = jnp.where(rw >= qiw + (nw - WIN_B), _dot_nt(qr, wcache_ref[:, :gw].astype(BF16)), NEG)
    s_wn = jnp.where(new_ok, _dot_nt(qr, pad_new(wnew_ref[:, :gw])), NEG)
    mxw = jnp.maximum(jnp.max(s_w, axis=-1, keepdims=True), jnp.max(s_wn, axis=-1, keepdims=True))
    p_w = jnp.exp(s_w - mxw)
    p_wn = jnp.exp(s_wn - mxw)
    den_w = jnp.sum(p_w, axis=-1, keepdims=True) + jnp.sum(p_wn, axis=-1, keepdims=True)
    res_w = (_dot(p_w.astype(BF16), wcache_ref[:, gw:].astype(BF16))
             + _dot(p_wn.astype(BF16), pad_new(wnew_ref[:, gw:]))) / den_w

    gt = jnp.broadcast_to(gates_ref[...][None], (N_HEADS, t_new, LANES)).reshape(nrow, LANES)
    rh = row_head(lax.broadcasted_iota(jnp.int32, (nrow, LANES), 0))

    def gate(branch):
        return jnp.sum(jnp.where(lane == 3 * rh + branch, gt, 0.0), axis=-1, keepdims=True)

    tot = gate(0) * res_c + gate(1) * res_s + gate(2) * res_w
    wide = jnp.where(hmask, _dot3(tot, foldt_ref[...]), 0.0)
    o_ref[...] = jnp.sum(wide.reshape(N_HEADS, t_new, D_MODEL), axis=0)


def _nsa_sample(q, qr, gates, kc, vc, rows_new, wcache, win_new, pool, table, consts, nb, t_new, npages):
    fold, foldt, cover, blk1h = consts
    gw = N_KV_B * HEAD_DIM
    n_c = kc.shape[1]
    nw = wcache.shape[1]
    tok = lambda w: pl.BlockSpec((None, t_new, w), lambda b, pt: (b, 0, 0))
    const = lambda a: pl.BlockSpec(a.shape, lambda b, pt: (0,) * a.ndim)

    def page_spec(k):
        return pl.BlockSpec((None, PAGE, 2 * gw), lambda b, pt: (pt[b * npages + k], 0, 1))

    grid_spec = pltpu.PrefetchScalarGridSpec(
        num_scalar_prefetch=1,
        grid=(nb,),
        in_specs=[tok(D_MODEL), tok(D_MODEL), tok(LANES),
                  pl.BlockSpec((None, n_c, gw), lambda b, pt: (b, 0, 0)),
                  pl.BlockSpec((None, n_c, gw), lambda b, pt: (b, 0, 0)),
                  pl.BlockSpec((None, t_new, 2 * gw), lambda b, pt: (b, 0, 1)),
                  pl.BlockSpec((None, nw, 2 * gw), lambda b, pt: (b, 0, 0)),
                  tok(2 * gw), const(fold), const(foldt), const(cover), const(blk1h)]
                 + [page_spec(k) for k in range(npages)],
        out_specs=pl.BlockSpec((None, t_new, D_MODEL), lambda b, pt: (b, 0, 0)),
    )
    kern = functools.partial(_nsa_sample_kernel, npages=npages, t_new=t_new,
                             n_c_valid=(npages * PAGE + t_new - CMP_LEN) // CMP_STRIDE + 1,
                             n_s_valid=-(-(npages * PAGE + t_new) // SLC_LEN))
    out = pl.pallas_call(
        kern, grid_spec=grid_spec,
        out_shape=jax.ShapeDtypeStruct((nb, t_new, D_MODEL), F32),
        compiler_params=_cparams(("arbitrary",)), name="nsa_sample",
    )(table, q.reshape(nb, t_new, D_MODEL), qr.reshape(nb, t_new, D_MODEL), gates.reshape(nb, t_new, LANES),
      kc, vc, rows_new.reshape(nb, t_new, 4 * gw), wcache, win_new.reshape(nb, t_new, 2 * gw),
      fold, foldt, cover, blk1h, *([pool] * npages))
    return out.reshape(nb * t_new, D_MODEL)


def _gate_expand(branch):
    r = jnp.arange(LANES)[:, None]
    c = jnp.arange(D_MODEL)[None, :]
    return (r == 3 * (c // HEAD_DIM) + branch).astype(BF16)


def _cover_matrix(n_c_rows, n_c_valid, n_s):
    c = jnp.arange(n_c_rows)[:, None]
    s = jnp.arange(LANES)[None, :]
    c_start = c * CMP_STRIDE
    s_start = s * SLC_LEN
    hit = (c_start < s_start + SLC_LEN) & (c_start + CMP_LEN > s_start) & (c < n_c_valid) & (s < n_s)
    return hit.astype(BF16)


def _fold_matrix():
    r = jnp.arange(D_MODEL)[:, None]
    c = jnp.arange(N_KV_B * HEAD_DIM)[None, :]
    return ((r // (HPG_B * HEAD_DIM) == c // HEAD_DIM) & (r % HEAD_DIM == c % HEAD_DIM)).astype(BF16)


def _prompt_mod_spec(tiles_per_batch):
    return pl.BlockSpec((None, 1, D_MODEL), lambda i, *_: (i // tiles_per_batch, 0, 0))


def _row_mod_spec(tm):
    return pl.BlockSpec((tm, D_MODEL), lambda i, *_: (i, 0))


def kernel(x_prompt, x_sample, c_prompt, c_sample, cache_a0, cache_a1, cache_a2, cache_b_pool, cache_b_win,
           page_table, g_norm, w_ada, b_ada, w_qkv_a, w_o_a, g_kv, w_ada_kv, b_ada_kv, w_kv_b, pe_cmp, w_cmp1,
           b_cmp1, w_cmp2, b_cmp2, w_qg_b, w_o_b, w_ffn_in, w_ffn_out, g_final):
    bsz, seq, d = x_prompt.shape
    nb, t_new, _ = x_sample.shape
    depth = w_ada.shape[0]
    n_a = w_qkv_a.shape[0]
    past = page_table.shape[1] * PAGE
    npages = page_table.shape[1]
    gw = N_KV_B * HEAD_DIM
    hq = N_HEADS * HEAD_DIM

    w_ada_b = w_ada.astype(BF16)
    w_ada_kv_b = w_ada_kv.astype(BF16)[None]
    w_qkv_b = w_qkv_a.astype(BF16)
    w_o_a_b = w_o_a.astype(BF16)
    w_kv_bb = w_kv_b.astype(BF16)
    w_q_b = w_qg_b[:, :, :hq].astype(BF16)
    w_g_b = jnp.pad(w_qg_b[:, :, hq:], ((0, 0), (0, 0), (0, LANES - 3 * N_HEADS))).astype(BF16)
    w_o_b_b = w_o_b.astype(BF16)
    w_in_b = w_ffn_in.astype(BF16)
    w_out_b = w_ffn_out.astype(BF16)
    cw = _compress_weights(pe_cmp, w_cmp1, b_cmp1, w_cmp2, b_cmp2)

    n_c_all = bsz + nb
    mp = -(-n_c_all // 8) * 8
    c_all = jnp.pad(jnp.concatenate([c_prompt, c_sample], axis=0), ((0, mp - n_c_all), (0, 0)))
    mod_all = _adaln(c_all, w_ada_b, b_ada[:, None, :], 6 * d // 4).reshape(depth, mp, 6, d)
    mkv_all = _adaln(c_all, w_ada_kv_b, b_ada_kv[None, None, :], d)[0].reshape(mp, 2, d)

    e_c, e_s, e_w = _gate_expand(0), _gate_expand(1), _gate_expand(2)
    dummy_tab = jnp.zeros((8, LANES), F32)
    dummy_spec = pl.BlockSpec((8, LANES), lambda i, j: (0, 0))

    def run_trunk(x, mods, mkv, mod_spec_of, tabs, tab_spec_of, tm, mix_a, make_shared, mix_b):
        cos, sin = tabs
        a_states = []
        shared = None
        shared_state = None
        for layer in range(depth):
            md = mods[layer]
            ms = mod_spec_of(tm)
            if layer == n_a:
                rows = _proj(x, mkv[0], mkv[1], ms, g_kv[None], w_kv_bb[:, :4 * gw], cos, sin, tab_spec_of(tm),
                             tm=tm, tn=256, mode="rule", rope_rule=(4, (2,)), name="nsa_kv_rows")
                winr = _proj(x, mkv[0], mkv[1], ms, g_kv[None], w_kv_bb[:, 4 * gw:], cos, sin, tab_spec_of(tm),
                             tm=tm, tn=256, mode="rule", rope_rule=(2, (0,)), name="nsa_kv_win")
                shared, shared_state = make_shared(rows, winr)
            gn = g_norm[layer, 0][None]
            if layer < n_a:
                qkv = [_proj(x, md[0], md[1], ms, gn, w_qkv_b[layer][:, 3 * g * d:3 * (g + 1) * d], cos, sin,
                             tab_spec_of(tm), tm=tm, tn=512, mode="rule", rope_rule=(12, tuple(range(8))),
                             name=f"qkv_a_g{g}") for g in range(len(A_PATTERNS))]
                acts, st = mix_a(layer, qkv)
                a_states.append(st)
                tmo = min(tm, 256)
                x = _oproj(acts, w_o_a_b[layer], x, md[2], mod_spec_of(tmo), tm=tmo, n_grp=3, name="oproj_a")
            else:
                lb = layer - n_a
                q, qr = _proj(x, md[0], md[1], ms, gn, w_q_b[lb], cos, sin, tab_spec_of(tm),
                              tm=tm, tn=512, mode="both", name="q_b")
                gates = _proj(x, md[0], md[1], ms, gn, w_g_b[lb], dummy_tab, dummy_tab, dummy_spec,
                              tm=tm, tn=LANES, mode="sigmoid", name="gates_b")
                o = mix_b(q, qr, gates, shared)
                tmo = min(tm, 512)
                x = _oproj([o], w_o_b_b[lb], x, md[2], mod_spec_of(tmo), tm=tmo, n_grp=0, name="oproj_b")
            x = _ffn(x, md[3], md[4], md[5], ms, g_norm[layer, 1][None], w_in_b[layer], w_out_b[layer],
                     tm=tm, tf=256, name="ffn")
        y = _final_norm(x, g_final[None], min(tm, 512))
        return y, a_states, shared_state

    tm_p = 1024
    mods_p = [[mod_all[l, :bsz, k][:, None, :] for k in range(6)] for l in range(depth)]
    mkv_p = [mkv_all[:bsz, k][:, None, :] for k in range(2)]
    tabs_p = _rope_tables(jnp.arange(seq))

    def mod_spec_p(tm):
        return _prompt_mod_spec(seq // tm)

    def tab_spec_p(tm):
        tpb = seq // tm
        return pl.BlockSpec((tm, LANES), lambda i, j: (i % tpb, 0))

    def mix_a_p(layer, qkv):
        q3 = [a.reshape(bsz, seq, -1) for a in qkv]
        outs, lses = [], []
        for grp in range(len(A_PATTERNS)):
            o, l = _dil_prompt(q3[grp], grp, bsz, seq)
            outs.append(o)
            lses.append(l)
        st = tuple(q3[g][:, seq - min(win, seq):, d:].reshape(bsz, min(win, seq), 2, N_HEADS, HEAD_DIM)
                   for g, (win, _) in enumerate(A_PATTERNS))
        return outs + lses, st

    n_c_p = (seq - CMP_LEN) // CMP_STRIDE + 1
    n_s_p = seq // SLC_LEN
    cover_p = _cover_matrix(seq // CMP_STRIDE, n_c_p, n_s_p)

    def shared_p(rows, winr):
        table = jnp.arange(bsz * (seq // PAGE), dtype=jnp.int32)
        kc, vc = _compress(rows.reshape(bsz * (seq // PAGE), PAGE, 4 * gw), table, bsz, seq // PAGE, cw)
        r5 = rows.reshape(bsz, seq, 4, N_KV_B, HEAD_DIM)
        blk1h = (jnp.arange(seq)[:, None] // SLC_LEN == jnp.arange(n_s_p)[None, :]).astype(BF16)
        ks = r5[:, :, 2].transpose(0, 2, 1, 3).astype(BF16)
        ka = jnp.concatenate([ks, jnp.broadcast_to(blk1h[None, None], (bsz, N_KV_B, seq, n_s_p))], axis=-1)
        pair = lambda a: a.reshape(bsz, a.shape[1], N_KV_B // 2, 2 * HEAD_DIM).transpose(0, 2, 1, 3).astype(BF16)

        def with_ones(a):
            v = a.reshape(bsz, a.shape[1], N_KV_B, HEAD_DIM).transpose(0, 2, 1, 3).astype(BF16)
            return jnp.concatenate([v, jnp.ones_like(v)], axis=-1)

        vs = with_ones(rows[:, 3 * gw:].reshape(bsz, seq, gw))
        w4 = winr.reshape(bsz, seq, 2, N_KV_B, HEAD_DIM)
        wpad = jnp.pad(winr.reshape(bsz, seq, 2 * gw), ((0, 0), (WIN_B, 0), (0, 0)))
        kw = pair(wpad[:, :, :gw])
        vw = with_ones(wpad[:, :, gw:])
        state = (r5, w4[:, seq - min(WIN_B, seq):])
        return (kc, vc, ka, vs, kw, vw), state

    def mix_b_p(q, qr, gates, shared):
        kc, vc, ka, vs, kw, vw = shared
        oc, sb = _cmp_prompt(q, gates, kc, vc, cover_p, e_c, bsz, seq)
        o = _selwin_prompt(qr, sb, gates, oc, ka, vs, kw, vw, e_s, e_w, bsz, seq)
        return o.reshape(bsz * seq, d)

    y_p, a_p, (rows_p, win_p) = run_trunk(x_prompt.reshape(bsz * seq, d), mods_p, mkv_p, mod_spec_p, tabs_p,
                                          tab_spec_p, tm_p, mix_a_p, shared_p, mix_b_p)

    m_s = nb * t_new
    tm_s = min(m_s, 512)
    mods_s = [[jnp.repeat(mod_all[l, bsz:bsz + nb, k], t_new, axis=0) for k in range(6)] for l in range(depth)]
    mkv_s = [jnp.repeat(mkv_all[bsz:bsz + nb, k], t_new, axis=0) for k in range(2)]
    cos_s, sin_s = _rope_tables(past + jnp.arange(t_new))
    tabs_s = (jnp.tile(cos_s, (nb, 1)), jnp.tile(sin_s, (nb, 1)))
    caches = [c.transpose(0, 1, 3, 4, 5, 2).reshape(c.shape[0], nb, 2 * d, c.shape[2])
              for c in (cache_a0, cache_a1, cache_a2)]
    a_state_s = [None] * len(A_PATTERNS)

    def tab_spec_s(tm):
        return pl.BlockSpec((tm, LANES), lambda i, j: (i, 0))

    def mix_a_s(layer, qkv):
        outs, lses = [], []
        for grp in range(len(A_PATTERNS)):
            o, l, a_state_s[grp] = _dil_sample(qkv[grp], caches[grp], a_state_s[grp], layer, grp, nb, t_new)
            outs.append(o)
            lses.append(l)
        return outs + lses, None

    n_c_s = (past + t_new - CMP_LEN) // CMP_STRIDE + 1
    n_s_s = -(-(past + t_new) // SLC_LEN)
    cover_s = _cover_matrix(past // CMP_STRIDE, n_c_s, n_s_s)
    fold = _fold_matrix()
    blk1h_s = (jnp.arange(LANES)[:, None] == jnp.arange(past + LANES)[None, :] // SLC_LEN).astype(BF16)
    table_s = page_table.reshape(-1).astype(jnp.int32)
    pool3 = cache_b_pool.reshape(cache_b_pool.shape[0], PAGE, 4 * gw)
    wcache = cache_b_win.reshape(nb, cache_b_win.shape[1], 2 * gw)

    def shared_s(rows, winr):
        kc, vc = _compress(pool3, table_s, nb, npages, cw)
        w4 = winr.reshape(nb, t_new, 2, N_KV_B, HEAD_DIM)
        win_full = jnp.concatenate([cache_b_win, w4], axis=1)
        n_keep = min(WIN_B, win_full.shape[1])
        state = (rows.reshape(nb, t_new, 4, N_KV_B, HEAD_DIM), win_full[:, win_full.shape[1] - n_keep:])
        return (kc, vc, rows, winr), state

    def mix_b_s(q, qr, gates, shared):
        kc, vc, rows, winr = shared
        return _nsa_sample(q, qr, gates, kc, vc, rows, wcache, winr, pool3, table_s,
                           (fold, fold.T, cover_s, blk1h_s), nb, t_new, npages)

    y_s, a_s, (rows_s, win_s) = run_trunk(x_sample.reshape(m_s, d), mods_s, mkv_s, _row_mod_spec, tabs_s,
                                          tab_spec_s, tm_s, mix_a_s, shared_s, mix_b_s)

    outs = [y_p.reshape(bsz, seq, d), y_s.reshape(nb, t_new, d)]
    for g, cache in enumerate((cache_a0, cache_a1, cache_a2)):
        outs.append(jnp.stack([st[g] for st in a_p]))
        st_t = a_state_s[g].reshape(cache.shape[0], nb, 2, N_HEADS, HEAD_DIM, cache.shape[2])
        outs.append(st_t.transpose(0, 1, 5, 2, 3, 4))
    outs += [rows_p, rows_s, win_p, win_s]
    return tuple(outs)
```

```python
import functools

import jax
import jax.numpy as jnp
from jax import lax
from jax.experimental import pallas as pl
from jax.experimental.pallas import tpu as pltpu

F32 = jnp.float32
BF16 = jnp.bfloat16

D_MODEL = 1024
HEAD_DIM = 64
N_HEADS = 16
A_PATTERNS = ((128, 1), (512, 4), (2048, 16))
N_KV_B = 4
HPG_B = 4
CMP_LEN = 32
CMP_STRIDE = 16
SLC_LEN = 64
N_SELECT = 16
WIN_B = 512
PAGE = 128
FORCE_SCORE = 1.0e4
ROPE_THETA = 10000.0
EPS = 1e-6
NEG = -1e30
Q_SCALE = HEAD_DIM ** -0.5
LANES = 128


def _cparams(sem, vmem_mb=48):
    return pltpu.CompilerParams(dimension_semantics=sem, vmem_limit_bytes=vmem_mb << 20)


def _dot(a, b):
    return jnp.dot(a, b, preferred_element_type=F32)


def _dot_nt(a, b):
    return lax.dot_general(a, b, (((1,), (1,)), ((), ())), preferred_element_type=F32)


def _split3(x):
    hi = x.astype(BF16)
    r1 = x - hi.astype(F32)
    mid = r1.astype(BF16)
    lo = (r1 - mid.astype(F32)).astype(BF16)
    return hi, mid, lo


def _dot3(x, m01):
    hi, mid, lo = _split3(x)
    return _dot(hi, m01) + _dot(mid, m01) + _dot(lo, m01)


def _norm_mod(x, g, sh, sc):
    ms = jnp.mean(x * x, axis=-1, keepdims=True)
    y = x * lax.rsqrt(ms + EPS) * g
    return y * (1.0 + sc) + sh


def _rope_tile(a, cos, sin):
    tn = a.shape[1]
    reps = tn // LANES
    if reps > 1:
        cos = jnp.concatenate([cos] * reps, axis=1)
        sin = jnp.concatenate([sin] * reps, axis=1)
    lane = lax.broadcasted_iota(jnp.int32, a.shape, 1)
    first = (lane % HEAD_DIM) < (HEAD_DIM // 2)
    rot = jnp.where(first, pltpu.roll(a, tn - HEAD_DIM // 2, 1), pltpu.roll(a, HEAD_DIM // 2, 1))
    return a * cos + rot * sin


def _rope_tables(pos):
    half = HEAD_DIM // 2
    inv_freq = ROPE_THETA ** (-jnp.arange(half, dtype=F32) / half)
    ang = pos.astype(F32)[:, None] * inv_freq[None, :]
    c, s = jnp.cos(ang), jnp.sin(ang)
    cos = jnp.tile(jnp.concatenate([c, c], axis=1), (1, LANES // HEAD_DIM))
    sin = jnp.tile(jnp.concatenate([-s, s], axis=1), (1, LANES // HEAD_DIM))
    return cos, sin


def _ada_kernel(c_ref, w_ref, b_ref, o_ref):
    c = c_ref[...]
    s = (c * jax.nn.sigmoid(c)).astype(BF16)
    o_ref[...] = _dot(s, w_ref[...]) + b_ref[...]


def _adaln(c, w, b, tn):
    mp, d = c.shape
    n_l, _, n = w.shape
    return pl.pallas_call(
        _ada_kernel,
        grid=(n_l, n // tn),
        in_specs=[pl.BlockSpec((mp, d), lambda l, j: (0, 0)),
                  pl.BlockSpec((None, d, tn), lambda l, j: (l, 0, j)),
                  pl.BlockSpec((None, 1, tn), lambda l, j: (l, 0, j))],
        out_specs=pl.BlockSpec((None, mp, tn), lambda l, j: (l, 0, j)),
        out_shape=jax.ShapeDtypeStruct((n_l, mp, n), F32),
        compiler_params=_cparams(("arbitrary", "arbitrary")),
        name="adaln",
    )(c, w, b)


def _proj_kernel(x_ref, sh_ref, sc_ref, g_ref, w_ref, cos_ref, sin_ref, *rest, mode, tn, rope_rule):
    n_out = 2 if mode == "both" else 1
    out_refs = rest[:n_out]
    h_scr = rest[n_out]
    j = pl.program_id(1)

    @pl.when(j == 0)
    def _():
        h_scr[...] = _norm_mod(x_ref[...], g_ref[...], sh_ref[...], sc_ref[...]).astype(BF16)

    acc = _dot(h_scr[...], w_ref[...])
    if mode == "sigmoid":
        out_refs[0][...] = jax.nn.sigmoid(acc)
    elif mode == "both":
        out_refs[0][...] = acc
        out_refs[1][...] = _rope_tile(acc, cos_ref[...], sin_ref[...])
    else:
        period, units = rope_rule
        unit = ((j * tn) // 256) % period
        flag = unit == units[0]
        for u in units[1:]:
            flag = jnp.logical_or(flag, unit == u)

        out_refs[0][...] = jnp.where(flag, _rope_tile(acc, cos_ref[...], sin_ref[...]), acc)


def _proj(x, sh, sc, mod_spec, g, w, cos, sin, tab_spec, *, tm, tn, mode, rope_rule=None, name):
    m, k = x.shape
    n = w.shape[1]
    n_out = 2 if mode == "both" else 1
    kern = functools.partial(_proj_kernel, mode=mode, tn=tn, rope_rule=rope_rule)
    out_spec = pl.BlockSpec((tm, tn), lambda i, j: (i, j))
    out = pl.pallas_call(
        kern,
        grid=(m // tm, n // tn),
        in_specs=[pl.BlockSpec((tm, k), lambda i, j: (i, 0)), mod_spec, mod_spec,
                  pl.BlockSpec((1, k), lambda i, j: (0, 0)),
                  pl.BlockSpec((k, tn), lambda i, j: (0, j)), tab_spec, tab_spec],
        out_specs=[out_spec] * n_out,
        out_shape=[jax.ShapeDtypeStruct((m, n), F32)] * n_out,
        scratch_shapes=[pltpu.VMEM((tm, k), BF16)],
        compiler_params=_cparams(("arbitrary", "arbitrary")),
        name=name,
    )(x, sh, sc, g, w, cos, sin)
    return out if n_out == 2 else out[0]


def _oproj_kernel(*refs, n_grp):
    if n_grp:
        o_refs, l_refs = refs[:n_grp], refs[n_grp:2 * n_grp]
        ls = [r[...] for r in l_refs]
        mx = ls[0]
        for l in ls[1:]:
            mx = jnp.maximum(mx, l)
        es = [jnp.exp(l - mx) for l in ls]
        den = es[0]
        num = es[0] * o_refs[0][...]
        for e, o in zip(es[1:], o_refs[1:]):
            den = den + e
            num = num + e * o[...]
        a = num / den
        rest = refs[2 * n_grp:]
    else:
        a = refs[0][...]
        rest = refs[1:]
    w_ref, x_ref, gt_ref, out_ref = rest
    out_ref[...] = x_ref[...] + gt_ref[...] * _dot(a.astype(BF16), w_ref[...])


def _oproj(acts, w, x, gate, mod_spec, *, tm, n_grp, name):
    m, d = x.shape
    k = w.shape[0]
    row = pl.BlockSpec((tm, k), lambda i: (i, 0))
    return pl.pallas_call(
        functools.partial(_oproj_kernel, n_grp=n_grp),
        grid=(m // tm,),
        in_specs=[row] * len(acts) + [pl.BlockSpec((k, d), lambda i: (0, 0)),
                                      pl.BlockSpec((tm, d), lambda i: (i, 0)), mod_spec],
        out_specs=pl.BlockSpec((tm, d), lambda i: (i, 0)),
        out_shape=jax.ShapeDtypeStruct((m, d), F32),
        compiler_params=_cparams(("arbitrary",)),
        name=name,
    )(*acts, w, x, gate)


def _ffn_kernel(x_ref, sh_ref, sc_ref, gt_ref, g_ref, wa_ref, wg_ref, wo_ref, out_ref, h_scr, acc_scr):
    f = pl.program_id(1)

    @pl.when(f == 0)
    def _():
        h_scr[...] = _norm_mod(x_ref[...], g_ref[...], sh_ref[...], sc_ref[...]).astype(BF16)
        acc_scr[...] = jnp.zeros_like(acc_scr)

    h = h_scr[...]
    a = _dot(h, wa_ref[...])
    gg = _dot(h, wg_ref[...])
    act = (a * jax.nn.sigmoid(a)) * gg
    acc_scr[...] += _dot(act.astype(BF16), wo_ref[...])

    @pl.when(f == pl.num_programs(1) - 1)
    def _():
        out_ref[...] = x_ref[...] + gt_ref[...] * acc_scr[...]


def _ffn(x, sh, sc, gt, mod_spec, g, w_in, w_out, *, tm, tf, name):
    m, d = x.shape
    dff = w_out.shape[0]
    nf = dff // tf
    return pl.pallas_call(
        _ffn_kernel,
        grid=(m // tm, nf),
        in_specs=[pl.BlockSpec((tm, d), lambda i, f: (i, 0)), mod_spec, mod_spec, mod_spec,
                  pl.BlockSpec((1, d), lambda i, f: (0, 0)),
                  pl.BlockSpec((d, tf), lambda i, f: (0, f)),
                  pl.BlockSpec((d, tf), lambda i, f: (0, f + nf)),
                  pl.BlockSpec((tf, d), lambda i, f: (f, 0))],
        out_specs=pl.BlockSpec((tm, d), lambda i, f: (i, 0)),
        out_shape=jax.ShapeDtypeStruct((m, d), F32),
        scratch_shapes=[pltpu.VMEM((tm, d), BF16), pltpu.VMEM((tm, d), F32)],
        compiler_params=_cparams(("arbitrary", "arbitrary")),
        name=name,
    )(x, sh, sc, gt, g, w_in, w_in, w_out)


def _final_norm_kernel(x_ref, g_ref, o_ref):
    x = x_ref[...]
    ms = jnp.mean(x * x, axis=-1, keepdims=True)
    o_ref[...] = x * lax.rsqrt(ms + EPS) * g_ref[...]


def _final_norm(x, g, tm):
    m, d = x.shape
    return pl.pallas_call(
        _final_norm_kernel,
        grid=(m // tm,),
        in_specs=[pl.BlockSpec((tm, d), lambda i: (i, 0)), pl.BlockSpec((1, d), lambda i: (0, 0))],
        out_specs=pl.BlockSpec((tm, d), lambda i: (i, 0)),
        out_shape=jax.ShapeDtypeStruct((m, d), F32),
        compiler_params=_cparams(("arbitrary",)),
        name="final_norm",
    )(x, g)


A_BLK = 128


def _dil_prompt_kernel(q_ref, kp_ref, kc_ref, vp_ref, vc_ref, o_ref, l_ref, s_scr, p_scr, den_scr):
    i = pl.program_id(2)
    row = lax.broadcasted_iota(jnp.int32, (A_BLK, A_BLK), 0)
    col = lax.broadcasted_iota(jnp.int32, (A_BLK, A_BLK), 1)
    m_prev = jnp.logical_and(col >= row, i > 0)
    m_cur = col <= row
    heads = [slice(h * HEAD_DIM, (h + 1) * HEAD_DIM) for h in range(N_HEADS)]
    for h, sl in enumerate(heads):
        q = (q_ref[:, sl] * Q_SCALE).astype(BF16)
        s_scr[h, :, :A_BLK] = _dot_nt(q, kp_ref[:, sl].astype(BF16))
        s_scr[h, :, A_BLK:] = _dot_nt(q, kc_ref[:, sl].astype(BF16))
    for h, sl in enumerate(heads):
        s_p = jnp.where(m_prev, s_scr[h, :, :A_BLK], NEG)
        s_c = jnp.where(m_cur, s_scr[h, :, A_BLK:], NEG)
        mx = jnp.max(jnp.maximum(s_p, s_c), axis=-1, keepdims=True)
        p_p = jnp.exp(s_p - mx)
        p_c = jnp.exp(s_c - mx)
        den = jnp.sum(p_p + p_c, axis=-1, keepdims=True)
        p_scr[h, :, :A_BLK] = p_p.astype(BF16)
        p_scr[h, :, A_BLK:] = p_c.astype(BF16)
        den_scr[h] = jnp.broadcast_to(den, (A_BLK, HEAD_DIM))
        l_ref[:, sl] = jnp.broadcast_to(mx + jnp.log(den), (A_BLK, HEAD_DIM))
    for h, sl in enumerate(heads):
        o = (_dot(p_scr[h, :, :A_BLK], vp_ref[:, sl].astype(BF16))
             + _dot(p_scr[h, :, A_BLK:], vc_ref[:, sl].astype(BF16)))
        o_ref[:, sl] = o / den_scr[h]


def _dil_prompt(qkv, grp, bsz, t):
    _, dil = A_PATTERNS[grp]
    tu = t // dil
    ncol = qkv.shape[-1] // D_MODEL
    qv = qkv.reshape(bsz, tu, dil * qkv.shape[-1])

    def spec(which, prev):
        if prev:
            return pl.BlockSpec((None, A_BLK, D_MODEL), lambda b, r, i: (b, jnp.maximum(i - 1, 0), r * ncol + which))
        return pl.BlockSpec((None, A_BLK, D_MODEL), lambda b, r, i: (b, i, r * ncol + which))

    out_spec = pl.BlockSpec((None, A_BLK, D_MODEL), lambda b, r, i: (b, i, r))
    shp = jax.ShapeDtypeStruct((bsz, tu, dil * D_MODEL), F32)
    o, l = pl.pallas_call(
        _dil_prompt_kernel,
        grid=(bsz, dil, tu // A_BLK),
        in_specs=[spec(0, False), spec(1, True), spec(1, False), spec(2, True), spec(2, False)],
        out_specs=[out_spec, out_spec],
        out_shape=[shp, shp],
        scratch_shapes=[pltpu.VMEM((N_HEADS, A_BLK, 2 * A_BLK), F32), pltpu.VMEM((N_HEADS, A_BLK, 2 * A_BLK), BF16),
                        pltpu.VMEM((N_HEADS, A_BLK, HEAD_DIM), F32)],
        compiler_params=_cparams(("arbitrary", "arbitrary", "arbitrary")),
        name=f"dil_prompt_g{grp}",
    )(qv, qv, qv, qv, qv)
    return o.reshape(bsz * t, D_MODEL), l.reshape(bsz * t, D_MODEL)


def _head_mask(rows, cols, row_head, col_div):
    r = lax.broadcasted_iota(jnp.int32, (rows, cols), 0)
    c = lax.broadcasted_iota(jnp.int32, (rows, cols), 1)
    return row_head(r) == c // col_div


def _dil_sample_kernel(q_ref, kn_ref, vn_ref, cur_ref, nxt_ref, *rest, dil, wc, t_new, aliased):
    if aliased:
        rest = rest[1:]
    o_ref, l_ref, st_ref, qx_scr, new_scr, m_scr, l_scr, acc_scr = rest
    c = pl.program_id(1)
    last = c == pl.num_programs(1) - 1
    nrow = N_HEADS * t_new
    hmask = _head_mask(nrow, D_MODEL, lambda r: r // t_new, HEAD_DIM)

    @pl.when(c == 0)
    def _():
        q = q_ref[...] * Q_SCALE
        qt = jnp.broadcast_to(q[None], (N_HEADS, t_new, D_MODEL)).reshape(nrow, D_MODEL)
        qx_scr[...] = jnp.where(hmask, qt, 0.0).astype(BF16)
        pad = jnp.zeros((LANES - t_new, D_MODEL), F32)
        new_scr[:D_MODEL, :] = jnp.concatenate([kn_ref[...], pad], axis=0).T
        new_scr[D_MODEL:, :] = jnp.concatenate([vn_ref[...], pad], axis=0).T
        m_scr[...] = jnp.full_like(m_scr, NEG)
        l_scr[...] = jnp.zeros_like(l_scr)
        acc_scr[...] = jnp.zeros_like(acc_scr)

    qx = qx_scr[...]

    def update(s, v_t):
        m_old = m_scr[...]
        m_new = jnp.maximum(m_old, jnp.max(s, axis=-1, keepdims=True))
        alpha = jnp.exp(m_old - m_new)
        p = jnp.exp(s - m_new)
        l_scr[...] = alpha * l_scr[...] + jnp.sum(p, axis=-1, keepdims=True)
        acc_scr[...] = alpha * acc_scr[...] + _dot_nt(p.astype(BF16), v_t)
        m_scr[...] = m_new

    cur = cur_ref[...]
    qi = lax.broadcasted_iota(jnp.int32, (nrow, wc), 0) % t_new
    rho = c * wc + lax.broadcasted_iota(jnp.int32, (nrow, wc), 1)
    valid = jnp.logical_and((rho - qi) % dil == 0, rho >= qi)
    update(jnp.where(valid, _dot(qx, cur[:D_MODEL].astype(BF16)), NEG), cur[D_MODEL:].astype(BF16))

    rolled = pltpu.roll(cur, wc - t_new, 1)
    tail = pltpu.roll(jnp.where(last, new_scr[...], nxt_ref[...]), LANES - t_new, 1)
    lane = lax.broadcasted_iota(jnp.int32, (2 * D_MODEL, LANES), 1)
    if wc > LANES:
        st_ref[:, :wc - LANES] = rolled[:, :wc - LANES]
    st_ref[:, wc - LANES:] = jnp.where(lane < LANES - t_new, rolled[:, wc - LANES:], tail)

    @pl.when(last)
    def _():
        qi2 = lax.broadcasted_iota(jnp.int32, (nrow, LANES), 0) % t_new
        kj = lax.broadcasted_iota(jnp.int32, (nrow, LANES), 1)
        ok = jnp.logical_and(kj <= qi2, (qi2 - kj) % dil == 0)
        update(jnp.where(ok, _dot(qx, new_scr[:D_MODEL, :].astype(BF16)), NEG), new_scr[D_MODEL:, :].astype(BF16))
        den = l_scr[...]
        res = jnp.where(hmask, acc_scr[...] / den, 0.0)
        lse = jnp.where(hmask, m_scr[...] + jnp.log(den), 0.0)
        o_ref[...] = jnp.sum(res.reshape(N_HEADS, t_new, D_MODEL), axis=0)
        l_ref[...] = jnp.sum(lse.reshape(N_HEADS, t_new, D_MODEL), axis=0)


def _dil_sample(qkv, cache_t, state, layer, grp, nb, t_new):
    win, dil = A_PATTERNS[grp]
    wc = min(win, 512)
    n_chunks = win // wc
    qv = qkv.reshape(nb, t_new, qkv.shape[-1])
    nrow = N_HEADS * t_new
    lanes_per_chunk = wc // LANES

    def new_spec(which):
        return pl.BlockSpec((None, t_new, D_MODEL), lambda b, c: (b, 0, which))

    def nxt_map(b, c):
        return (layer, b, 0, jnp.minimum((c + 1) * lanes_per_chunk, win // LANES - 1))

    out_spec = pl.BlockSpec((None, t_new, D_MODEL), lambda b, c: (b, 0, 0))
    shp = jax.ShapeDtypeStruct((nb, t_new, D_MODEL), F32)
    in_specs = [new_spec(0), new_spec(1), new_spec(2),
                pl.BlockSpec((None, None, 2 * D_MODEL, wc), lambda b, c: (layer, b, 0, c)),
                pl.BlockSpec((None, None, 2 * D_MODEL, LANES), nxt_map)]
    args = [qv, qv, qv, cache_t, cache_t]
    aliases = {}
    if state is not None:
        in_specs.append(pl.BlockSpec(memory_space=pl.ANY))
        args.append(state)
        aliases = {len(args) - 1: 2}
    o, l, st = pl.pallas_call(
        functools.partial(_dil_sample_kernel, dil=dil, wc=wc, t_new=t_new, aliased=state is not None),
        grid=(nb, n_chunks),
        in_specs=in_specs,
        out_specs=[out_spec, out_spec,
                   pl.BlockSpec((None, None, 2 * D_MODEL, wc), lambda b, c: (layer, b, 0, c))],
        out_shape=[shp, shp, jax.ShapeDtypeStruct(cache_t.shape, F32)],
        scratch_shapes=[pltpu.VMEM((nrow, D_MODEL), BF16), pltpu.VMEM((2 * D_MODEL, LANES), F32),
                        pltpu.VMEM((nrow, 1), F32), pltpu.VMEM((nrow, 1), F32), pltpu.VMEM((nrow, D_MODEL), F32)],
        input_output_aliases=aliases,
        compiler_params=_cparams(("arbitrary", "arbitrary"), 56),
        name=f"dil_sample_g{grp}",
    )(*args)
    return o.reshape(nb * t_new, D_MODEL), l.reshape(nb * t_new, D_MODEL), st


CMP_SUB = PAGE // CMP_STRIDE


CMP_PAGES = 4
CMP_CHUNKS = 4


def _compress_kernel(*refs, pages_t):
    n_pg = CMP_PAGES * CMP_CHUNKS
    page_refs = refs[1:1 + n_pg]
    pelo_ref, pehi_ref, w1lo_ref, w1hi_ref, b1_ref, w2_ref, b2_ref, kc_ref, vc_ref, zlo, zhi = refs[1 + n_pg:][:11]
    k = pl.program_id(1)
    nsteps = pl.num_programs(1)
    nu = zlo.shape[1]
    for pp in range(CMP_PAGES):
        r0 = pl.multiple_of((k * CMP_PAGES + pp) * CMP_SUB, CMP_SUB)
        for m in range(CMP_CHUNKS):
            page_ref = page_refs[pp * CMP_CHUNKS + m]
            if pages_t:
                tr_scr = refs[-1]
                tr_scr[pp * CMP_CHUNKS + m] = page_ref[...].T
                page_ref = tr_scr.at[pp * CMP_CHUNKS + m]
            for p in range(CMP_STRIDE):
                rows = page_ref[pl.ds(p, CMP_SUB, stride=CMP_STRIDE), :]
                lanes = slice(p * LANES, (p + 1) * LANES)
                zlo[m, pl.ds(r0, CMP_SUB), lanes] = rows + pelo_ref[p:p + 1, m * LANES:(m + 1) * LANES]
                zhi[m, pl.ds(r0, CMP_SUB), lanes] = rows + pehi_ref[p:p + 1, m * LANES:(m + 1) * LANES]

    @pl.when(k == nsteps - 1)
    def _():
        for m in range(4):
            ty = m // 2
            first = _dot(zlo[m].astype(BF16), w1lo_ref[ty])
            second = _dot(zhi[m].astype(BF16), w1hi_ref[ty])
            hid = first + pltpu.roll(second, nu - 1, 0) + b1_ref[ty]
            out = _dot(jax.nn.gelu(hid).astype(BF16), w2_ref[ty]) + b2_ref[ty]
            dst = kc_ref if ty == 0 else vc_ref
            dst[:, (m % 2) * LANES:(m % 2 + 1) * LANES] = out


def _compress(pages, table, nb, npages, cw, pages_t=False):
    pelo, pehi, w1lo, w1hi, b1, w2, b2 = cw
    nu = npages * CMP_SUB
    const2 = lambda shape: pl.BlockSpec(shape, lambda b, k, pt: (0,) * len(shape))

    def chunk_spec(pp, m):
        page = lambda b, k, pt: pt[b * npages + k * CMP_PAGES + pp]
        if pages_t:
            return pl.BlockSpec((None, LANES, PAGE), lambda b, k, pt: (page(b, k, pt), m, 0))
        return pl.BlockSpec((None, PAGE, LANES), lambda b, k, pt: (page(b, k, pt), 0, m))

    n_pg = CMP_PAGES * CMP_CHUNKS
    tr_scratch = [pltpu.VMEM((n_pg, PAGE, LANES), F32)] if pages_t else []
    grid_spec = pltpu.PrefetchScalarGridSpec(
        num_scalar_prefetch=1,
        grid=(nb, npages // CMP_PAGES),
        in_specs=[chunk_spec(pp, m) for pp in range(CMP_PAGES) for m in range(CMP_CHUNKS)]
                 + [const2(pelo.shape), const2(pehi.shape), const2(w1lo.shape), const2(w1hi.shape),
                    const2(b1.shape), const2(w2.shape), const2(b2.shape)],
        out_specs=[pl.BlockSpec((None, nu, N_KV_B * HEAD_DIM), lambda b, k, pt: (b, 0, 0))] * 2,
        scratch_shapes=[pltpu.VMEM((4, nu, CMP_STRIDE * LANES), F32)] * 2 + tr_scratch,
    )
    shp = jax.ShapeDtypeStruct((nb, nu, N_KV_B * HEAD_DIM), F32)
    return pl.pallas_call(
        functools.partial(_compress_kernel, pages_t=pages_t), grid_spec=grid_spec, out_shape=[shp, shp],
        compiler_params=_cparams(("arbitrary", "arbitrary")), name="nsa_compress",
    )(table, *([pages] * n_pg), pelo, pehi, w1lo, w1hi, b1, w2, b2)


def _compress_weights(pe_cmp, w_cmp1, b_cmp1, w_cmp2, b_cmp2):
    eye2 = jnp.eye(2, dtype=F32)
    hid = w_cmp1.shape[-1]
    w1 = w_cmp1.reshape(2, CMP_LEN, HEAD_DIM, hid)

    def pair_w1(w):
        return jnp.einsum("pen,ab->paebn", w, eye2).reshape(CMP_STRIDE * LANES, 2 * hid)

    w1lo = jnp.stack([pair_w1(w1[ty, :CMP_STRIDE]) for ty in range(2)]).astype(BF16)
    w1hi = jnp.stack([pair_w1(w1[ty, CMP_STRIDE:]) for ty in range(2)]).astype(BF16)
    w2 = jnp.stack([jnp.einsum("ne,ab->anbe", w_cmp2[ty], eye2).reshape(2 * hid, 2 * HEAD_DIM)
                    for ty in range(2)]).astype(BF16)
    b1 = jnp.tile(b_cmp1, (1, 2))[:, None, :]
    b2 = jnp.tile(b_cmp2, (1, 2))[:, None, :]
    pelo = jnp.concatenate([jnp.tile(pe_cmp[ty, :CMP_STRIDE], (1, N_KV_B)) for ty in range(2)], axis=1)
    pehi = jnp.concatenate([jnp.tile(pe_cmp[ty, CMP_STRIDE:], (1, N_KV_B)) for ty in range(2)], axis=1)
    return pelo, pehi, w1lo, w1hi, b1, w2, b2


C_TQ = 128


def _rank_select(imp_t, n_blk):
    sidx = lax.broadcasted_iota(jnp.int32, imp_t.shape, 0)
    cnt = jnp.zeros(imp_t.shape, jnp.int32)
    for s in range(n_blk):
        row = imp_t[s:s + 1, :]
        ahead = jnp.logical_or(row > imp_t, jnp.logical_and(row == imp_t, sidx > s))
        cnt = cnt + ahead.astype(jnp.int32)
    return cnt < N_SELECT


def _cmp_prompt_kernel(q_ref, gates_ref, kc_ref, vc_ref, cover_ref, ec_ref, oc_ref, sb_ref):
    i = pl.program_id(1)
    n_c = kc_ref.shape[0]
    n_s = sb_ref.shape[-1]
    t = i * C_TQ + lax.broadcasted_iota(jnp.int32, (C_TQ, n_c), 0)
    c_idx = lax.broadcasted_iota(jnp.int32, (C_TQ, n_c), 1)
    valid = c_idx * CMP_STRIDE + (CMP_LEN - 1) <= t
    any_valid = (t[:, :1] >= CMP_LEN - 1).astype(F32)
    tq = i * C_TQ + lax.broadcasted_iota(jnp.int32, (C_TQ, LANES), 0)
    jb = lax.broadcasted_iota(jnp.int32, (C_TQ, LANES), 1)
    cur = tq // SLC_LEN
    forced = jnp.logical_or(jb == 0, jnp.logical_or(jb == cur, jb == cur - 1))
    cover = cover_ref[...]
    gates = gates_ref[...]
    gh = gates.astype(BF16)
    gl = (gates - gh.astype(F32)).astype(BF16)
    gfull = _dot(gh, ec_ref[...]) + _dot(gl, ec_ref[...])
    for g in range(N_KV_B):
        gs = slice(g * HEAD_DIM, (g + 1) * HEAD_DIM)
        kc = kc_ref[:, gs].astype(BF16)
        vc = vc_ref[:, gs].astype(BF16)
        psum = jnp.zeros((C_TQ, n_c), F32)
        for j in range(HPG_B):
            h = g * HPG_B + j
            hs = slice(h * HEAD_DIM, (h + 1) * HEAD_DIM)
            q = (q_ref[:, hs] * Q_SCALE).astype(BF16)
            s = jnp.where(valid, _dot_nt(q, kc), NEG)
            mx = jnp.max(s, axis=-1, keepdims=True)
            e = jnp.exp(s - mx)
            p = e / jnp.sum(e, axis=-1, keepdims=True) * any_valid
            psum = psum + p
            oc_ref[:, hs] = _dot(p.astype(BF16), vc) * gfull[:, hs]
        imp = _dot3(psum, cover)
        imp = jnp.where(forced, FORCE_SCORE, jnp.where(jb <= cur, imp, -1.0))
        sel_t = _rank_select(imp.T[:n_s, :], n_s)
        bias_t = jnp.where(sel_t, 0.0, NEG)
        bias_t = jnp.concatenate([bias_t, jnp.zeros((LANES - n_s, C_TQ), F32)], axis=0)
        sb_ref[g] = bias_t.T[:, :n_s].astype(BF16)


def _cmp_prompt(q, gates, kc, vc, cover, e_c, bsz, t):
    n_c = kc.shape[1]
    n_s = t // SLC_LEN
    qv = q.reshape(bsz, t, D_MODEL)
    gv = gates.reshape(bsz, t, LANES)
    oc, sb = pl.pallas_call(
        _cmp_prompt_kernel,
        grid=(bsz, t // C_TQ),
        in_specs=[pl.BlockSpec((None, C_TQ, D_MODEL), lambda b, i: (b, i, 0)),
                  pl.BlockSpec((None, C_TQ, LANES), lambda b, i: (b, i, 0)),
                  pl.BlockSpec((None, n_c, N_KV_B * HEAD_DIM), lambda b, i: (b, 0, 0)),
                  pl.BlockSpec((None, n_c, N_KV_B * HEAD_DIM), lambda b, i: (b, 0, 0)),
                  pl.BlockSpec(cover.shape, lambda b, i: (0, 0)),
                  pl.BlockSpec(e_c.shape, lambda b, i: (0, 0))],
        out_specs=[pl.BlockSpec((None, C_TQ, D_MODEL), lambda b, i: (b, i, 0)),
                   pl.BlockSpec((None, N_KV_B, C_TQ, n_s), lambda b, i: (b, 0, i, 0))],
        out_shape=[jax.ShapeDtypeStruct((bsz, t, D_MODEL), F32),
                   jax.ShapeDtypeStruct((bsz, N_KV_B, t, n_s), BF16)],
        compiler_params=_cparams(("arbitrary", "arbitrary")),
        name="nsa_cmp_prompt",
    )(qv, gv, kc, vc, cover, e_c)
    return oc, sb


S_TQ = 128
S_TK = 256
W_TK = 128
S_GROUP_SETS = ((0, 1, 2, 3),)


def _selwin_prompt_kernel(qr_ref, sb_ref, gates_ref, oc_ref, ka_ref, vs_ref, kw_ref, vw_ref, es_ref, ew_ref, o_ref,
                          s_scr):
    i = pl.program_id(1)
    t0 = i * S_TQ
    nrow = HPG_B * S_TQ
    qr = qr_ref[...] * Q_SCALE
    tpos = t0 + lax.broadcasted_iota(jnp.int32, (nrow, 1), 0) % S_TQ
    zeros = jnp.zeros((nrow, HEAD_DIM), BF16)
    q_aug, q_win = [], []
    for g in range(N_KV_B):
        heads = [qr[:, (g * HPG_B + j) * HEAD_DIM:(g * HPG_B + j + 1) * HEAD_DIM] for j in range(HPG_B)]
        qa = jnp.concatenate(heads, axis=0).astype(BF16)
        q_aug.append(jnp.concatenate([qa, jnp.concatenate([sb_ref[g]] * HPG_B, axis=0)], axis=1))
        q_win.append(jnp.concatenate([qa, zeros] if g % 2 == 0 else [zeros, qa], axis=1))

    def online(carry, s, v):
        m_old, acc = carry
        m_new = jnp.maximum(m_old, jnp.max(s, axis=-1, keepdims=True))
        p = jnp.exp(s - m_new)
        return m_new, jnp.exp(m_old - m_new) * acc + _dot(p.astype(BF16), v)

    groups = tuple(range(N_KV_B))

    def scores(kt, g):
        return _dot_nt(q_aug[g], ka_ref[g, pl.ds(pl.multiple_of(kt * S_TK, S_TK), S_TK), :])

    def sel_step(kt, carry, diagonal, gset):
        k0 = pl.multiple_of(kt * S_TK, S_TK)
        out = []
        for c, g in zip(carry, gset):
            s = s_scr[g]
            if diagonal:
                kpos = k0 + lax.broadcasted_iota(jnp.int32, (nrow, S_TK), 1)
                s = jnp.where(kpos <= tpos, s, NEG)
            else:
                s_scr[g] = scores(kt + 1, g)
            out.append(online(c, s, vs_ref[g, pl.ds(k0, S_TK), :]))
        return tuple(out)

    n_full = t0 // S_TK
    sel = [None] * N_KV_B
    for gset in S_GROUP_SETS:
        init = tuple((jnp.full((nrow, 1), NEG, F32), jnp.zeros((nrow, 2 * HEAD_DIM), F32)) for _ in gset)
        for g in gset:
            s_scr[g] = scores(0, g)
        part = lax.fori_loop(0, n_full, lambda kt, c, gset=gset: sel_step(kt, c, False, gset), init)
        part = sel_step(n_full, part, True, gset)
        for c, g in zip(part, gset):
            sel[g] = c

    n_win = (WIN_B + S_TQ) // W_TK
    win = []
    for g in groups:
        tiles = []
        for w in range(n_win):
            k0 = pl.multiple_of(t0 + w * W_TK, SLC_LEN)
            kpos = k0 - WIN_B + lax.broadcasted_iota(jnp.int32, (nrow, W_TK), 1)
            ok = jnp.logical_and(kpos >= 0, jnp.logical_and(kpos <= tpos, tpos - kpos <= WIN_B))
            tiles.append(jnp.where(ok, _dot_nt(q_win[g], kw_ref[g // 2, pl.ds(k0, W_TK), :]), NEG))
        mx = jnp.max(tiles[0], axis=-1, keepdims=True)
        for s in tiles[1:]:
            mx = jnp.maximum(mx, jnp.max(s, axis=-1, keepdims=True))
        acc = jnp.zeros((nrow, 2 * HEAD_DIM), F32)
        for w, s in enumerate(tiles):
            k0 = pl.multiple_of(t0 + w * W_TK, SLC_LEN)
            acc = acc + _dot(jnp.exp(s - mx).astype(BF16), vw_ref[g, pl.ds(k0, W_TK), :])
        win.append((mx, acc))

    gates = gates_ref[...]
    gh = gates.astype(BF16)
    gl = (gates - gh.astype(F32)).astype(BF16)
    g_s = _dot(gh, es_ref[...]) + _dot(gl, es_ref[...])
    g_w = _dot(gh, ew_ref[...]) + _dot(gl, ew_ref[...])
    for g in groups:
        o_s = sel[g][1][:, :HEAD_DIM] / sel[g][1][:, HEAD_DIM:HEAD_DIM + 1]
        o_w = win[g][1][:, :HEAD_DIM] / win[g][1][:, HEAD_DIM:HEAD_DIM + 1]
        for j in range(HPG_B):
            hs = slice((g * HPG_B + j) * HEAD_DIM, (g * HPG_B + j + 1) * HEAD_DIM)
            rs = slice(j * S_TQ, (j + 1) * S_TQ)
            o_ref[:, hs] = oc_ref[:, hs] + g_s[:, hs] * o_s[rs, :] + g_w[:, hs] * o_w[rs, :]


def _selwin_prompt(qr, sb, gates, oc, ka, vs, kw, vw, e_s, e_w, bsz, t):
    qv = qr.reshape(bsz, t, D_MODEL)
    gv = gates.reshape(bsz, t, LANES)
    n_s = sb.shape[-1]
    tp = kw.shape[2]
    tok = pl.BlockSpec((None, S_TQ, D_MODEL), lambda b, i: (b, i, 0))
    resident = lambda a: pl.BlockSpec((None,) + a.shape[1:], lambda b, i: (b, 0, 0, 0))
    return pl.pallas_call(
        _selwin_prompt_kernel,
        grid=(bsz, t // S_TQ),
        in_specs=[tok,
                  pl.BlockSpec((None, N_KV_B, S_TQ, n_s), lambda b, i: (b, 0, i, 0)),
                  pl.BlockSpec((None, S_TQ, LANES), lambda b, i: (b, i, 0)),
                  tok, resident(ka), resident(vs), resident(kw), resident(vw),
                  pl.BlockSpec(e_s.shape, lambda b, i: (0, 0)),
                  pl.BlockSpec(e_w.shape, lambda b, i: (0, 0))],
        out_specs=tok,
        out_shape=jax.ShapeDtypeStruct((bsz, t, D_MODEL), F32),
        scratch_shapes=[pltpu.VMEM((N_KV_B, HPG_B * S_TQ, S_TK), F32)],
        compiler_params=_cparams(("arbitrary", "arbitrary"), 56),
        name="nsa_selwin_prompt",
    )(qv, sb, gv, oc, ka, vs, kw, vw, e_s, e_w)


def _nsa_sample_kernel(pt_ref, q_ref, qr_ref, gates_ref, kc_ref, vc_ref, newrows_ref, wcache_ref, wnew_ref,
                       fold_ref, foldt_ref, cover_ref, blk1h_ref, *rest, npages, t_new, n_c_valid, n_s_valid):
    del pt_ref
    page_refs = rest[:npages]
    o_ref = rest[npages]
    nrow = N_HEADS * t_new
    gw = N_KV_B * HEAD_DIM
    row_head = lambda r: ((r // t_new) % N_KV_B) * HPG_B + r // (t_new * N_KV_B)
    hmask = _head_mask(nrow, D_MODEL, row_head, HEAD_DIM)
    fold = fold_ref[...]

    def qexp(ref):
        q = ref[...] * Q_SCALE
        qt = jnp.broadcast_to(q[None], (N_HEADS, t_new, D_MODEL)).reshape(nrow, D_MODEL)
        return _dot(jnp.where(hmask, qt, 0.0).astype(BF16), fold).astype(BF16)

    qc = qexp(q_ref)
    qr = qexp(qr_ref)
    qi = lax.broadcasted_iota(jnp.int32, (nrow, LANES), 0) % t_new
    lane = lax.broadcasted_iota(jnp.int32, (nrow, LANES), 1)
    new_ok = lane <= qi

    def pad_new(x):
        return jnp.concatenate([x, jnp.zeros((LANES - t_new, x.shape[1]), F32)], axis=0).astype(BF16)

    kc = kc_ref[...].astype(BF16)
    s = jnp.where(lane < n_c_valid, _dot_nt(qc, kc), NEG)
    e = jnp.exp(s - jnp.max(s, axis=-1, keepdims=True))
    p_c = e / jnp.sum(e, axis=-1, keepdims=True)
    res_c = _dot(p_c.astype(BF16), vc_ref[...].astype(BF16))

    ng = N_KV_B * t_new
    psum = p_c[0:ng] + p_c[ng:2 * ng] + p_c[2 * ng:3 * ng] + p_c[3 * ng:4 * ng]
    imp = _dot3(psum, cover_ref[...])
    jb = lax.broadcasted_iota(jnp.int32, (ng, LANES), 1)
    cur = n_s_valid - 1
    forced = jnp.logical_or(jb == 0, jnp.logical_or(jb == cur, jb == cur - 1))
    imp = jnp.where(forced, FORCE_SCORE, jnp.where(jb <= cur, imp, -2.0))
    cnt = jnp.zeros((ng, LANES), jnp.int32)
    for sblk in range(n_s_valid):
        colv = jnp.broadcast_to(imp[:, sblk:sblk + 1], (ng, LANES))
        ahead = jnp.logical_or(colv > imp, jnp.logical_and(colv == imp, jb > sblk))
        cnt = cnt + ahead.astype(jnp.int32)
    bias_g = jnp.where(cnt < N_SELECT, 0.0, NEG).astype(BF16)
    bias = jnp.concatenate([bias_g] * HPG_B, axis=0)
    key_bias = _dot(bias, blk1h_ref[...])

    past = npages * PAGE
    s_parts = [_dot(qr, page_refs[k][:gw, :].astype(BF16)) for k in range(npages)]
    s_new = jnp.where(new_ok, _dot_nt(qr, pad_new(newrows_ref[:, :gw])), NEG)
    s_all = jnp.concatenate(s_parts + [s_new], axis=1) + key_bias
    mx = jnp.max(s_all, axis=-1, keepdims=True)
    p_s = jnp.exp(s_all - mx)
    den_s = jnp.sum(p_s, axis=-1, keepdims=True)
    p_sb = p_s.astype(BF16)
    res_s = _dot(p_sb[:, past:], pad_new(newrows_ref[:, gw:]))
    for k in range(npages):
        res_s = res_s + _dot_nt(p_sb[:, k * PAGE:(k + 1) * PAGE], page_refs[k][gw:, :].astype(BF16))
    res_s = res_s / den_s

    nw = wcache_ref.shape[0]
    qiw = lax.broadcasted_iota(jnp.int32, (nrow, nw), 0) % t_new
    rw = lax.broadcasted_iota(jnp.int32, (nrow, nw), 1)
    s_w = jnp.where(rw >= qiw + (nw - WIN_B), _dot_nt(qr, wcache_ref[:, :gw].astype(BF16)), NEG)
    s_wn = jnp.where(new_ok, _dot_nt(qr, pad_new(wnew_ref[:, :gw])), NEG)
    mxw = jnp.maximum(jnp.max(s_w, axis=-1, keepdims=True), jnp.max(s_wn, axis=-1, keepdims=True))
    p_w = jnp.exp(s_w - mxw)
    p_wn = jnp.exp(s_wn - mxw)
    den_w = jnp.sum(p_w, axis=-1, keepdims=True) + jnp.sum(p_wn, axis=-1, keepdims=True)
    res_w = (_dot(p_w.astype(BF16), wcache_ref[:, gw:].astype(BF16))
             + _dot(p_wn.astype(BF16), pad_new(wnew_ref[:, gw:]))) / den_w

    gt = jnp.broadcast_to(gates_ref[...][None], (N_HEADS, t_new, LANES)).reshape(nrow, LANES)
    rh = row_head(lax.broadcasted_iota(jnp.int32, (nrow, LANES), 0))

    def gate(branch):
        return jnp.sum(jnp.where(lane == 3 * rh + branch, gt, 0.0), axis=-1, keepdims=True)

    tot = gate(0) * res_c + gate(1) * res_s + gate(2) * res_w
    wide = jnp.where(hmask, _dot3(tot, foldt_ref[...]), 0.0)
    o_ref[...] = jnp.sum(wide.reshape(N_HEADS, t_new, D_MODEL), axis=0)


def _nsa_sample(q, qr, gates, kc, vc, rows_new, wcache, win_new, pool, table, consts, nb, t_new, npages):
    fold, foldt, cover, blk1h = consts
    gw = N_KV_B * HEAD_DIM
    n_c = kc.shape[1]
    nw = wcache.shape[1]
    tok = lambda w: pl.BlockSpec((None, t_new, w), lambda b, pt: (b, 0, 0))
    const = lambda a: pl.BlockSpec(a.shape, lambda b, pt: (0,) * a.ndim)

    def page_spec(k):
        return pl.BlockSpec((None, 2 * gw, PAGE), lambda b, pt: (pt[b * npages + k], 1, 0))

    grid_spec = pltpu.PrefetchScalarGridSpec(
        num_scalar_prefetch=1,
        grid=(nb,),
        in_specs=[tok(D_MODEL), tok(D_MODEL), tok(LANES),
                  pl.BlockSpec((None, n_c, gw), lambda b, pt: (b, 0, 0)),
                  pl.BlockSpec((None, n_c, gw), lambda b, pt: (b, 0, 0)),
                  pl.BlockSpec((None, t_new, 2 * gw), lambda b, pt: (b, 0, 1)),
                  pl.BlockSpec((None, nw, 2 * gw), lambda b, pt: (b, 0, 0)),
                  tok(2 * gw), const(fold), const(foldt), const(cover), const(blk1h)]
                 + [page_spec(k) for k in range(npages)],
        out_specs=pl.BlockSpec((None, t_new, D_MODEL), lambda b, pt: (b, 0, 0)),
    )
    kern = functools.partial(_nsa_sample_kernel, npages=npages, t_new=t_new,
                             n_c_valid=(npages * PAGE + t_new - CMP_LEN) // CMP_STRIDE + 1,
                             n_s_valid=-(-(npages * PAGE + t_new) // SLC_LEN))
    out = pl.pallas_call(
        kern, grid_spec=grid_spec,
        out_shape=jax.ShapeDtypeStruct((nb, t_new, D_MODEL), F32),
        compiler_params=_cparams(("arbitrary",)), name="nsa_sample",
    )(table, q.reshape(nb, t_new, D_MODEL), qr.reshape(nb, t_new, D_MODEL), gates.reshape(nb, t_new, LANES),
      kc, vc, rows_new.reshape(nb, t_new, 4 * gw), wcache, win_new.reshape(nb, t_new, 2 * gw),
      fold, foldt, cover, blk1h, *([pool] * npages))
    return out.reshape(nb * t_new, D_MODEL)


def _gate_expand(branch):
    r = jnp.arange(LANES)[:, None]
    c = jnp.arange(D_MODEL)[None, :]
    return (r == 3 * (c // HEAD_DIM) + branch).astype(BF16)


def _cover_matrix(n_c_rows, n_c_valid, n_s):
    c = jnp.arange(n_c_rows)[:, None]
    s = jnp.arange(LANES)[None, :]
    c_start = c * CMP_STRIDE
    s_start = s * SLC_LEN
    hit = (c_start < s_start + SLC_LEN) & (c_start + CMP_LEN > s_start) & (c < n_c_valid) & (s < n_s)
    return hit.astype(BF16)


def _fold_matrix():
    r = jnp.arange(D_MODEL)[:, None]
    c = jnp.arange(N_KV_B * HEAD_DIM)[None, :]
    return ((r // (HPG_B * HEAD_DIM) == c // HEAD_DIM) & (r % HEAD_DIM == c % HEAD_DIM)).astype(BF16)


def _prompt_mod_spec(tiles_per_batch):
    return pl.BlockSpec((None, 1, D_MODEL), lambda i, *_: (i // tiles_per_batch, 0, 0))


def _row_mod_spec(tm):
    return pl.BlockSpec((tm, D_MODEL), lambda i, *_: (i, 0))


def kernel(x_prompt, x_sample, c_prompt, c_sample, cache_a0, cache_a1, cache_a2, cache_b_pool, cache_b_win,
           page_table, g_norm, w_ada, b_ada, w_qkv_a, w_o_a, g_kv, w_ada_kv, b_ada_kv, w_kv_b, pe_cmp, w_cmp1,
           b_cmp1, w_cmp2, b_cmp2, w_qg_b, w_o_b, w_ffn_in, w_ffn_out, g_final):
    bsz, seq, d = x_prompt.shape
    nb, t_new, _ = x_sample.shape
    depth = w_ada.shape[0]
    n_a = w_qkv_a.shape[0]
    past = page_table.shape[1] * PAGE
    npages = page_table.shape[1]
    gw = N_KV_B * HEAD_DIM
    hq = N_HEADS * HEAD_DIM

    w_ada_b = w_ada.astype(BF16)
    w_ada_kv_b = w_ada_kv.astype(BF16)[None]
    w_qkv_b = w_qkv_a.astype(BF16)
    w_o_a_b = w_o_a.astype(BF16)
    w_kv_bb = w_kv_b.astype(BF16)
    w_q_b = w_qg_b[:, :, :hq].astype(BF16)
    w_g_b = jnp.pad(w_qg_b[:, :, hq:], ((0, 0), (0, 0), (0, LANES - 3 * N_HEADS))).astype(BF16)
    w_o_b_b = w_o_b.astype(BF16)
    w_in_b = w_ffn_in.astype(BF16)
    w_out_b = w_ffn_out.astype(BF16)
    cw = _compress_weights(pe_cmp, w_cmp1, b_cmp1, w_cmp2, b_cmp2)

    n_c_all = bsz + nb
    mp = -(-n_c_all // 8) * 8
    c_all = jnp.pad(jnp.concatenate([c_prompt, c_sample], axis=0), ((0, mp - n_c_all), (0, 0)))
    mod_all = _adaln(c_all, w_ada_b, b_ada[:, None, :], 6 * d // 4).reshape(depth, mp, 6, d)
    mkv_all = _adaln(c_all, w_ada_kv_b, b_ada_kv[None, None, :], d)[0].reshape(mp, 2, d)

    e_c, e_s, e_w = _gate_expand(0), _gate_expand(1), _gate_expand(2)
    dummy_tab = jnp.zeros((8, LANES), F32)
    dummy_spec = pl.BlockSpec((8, LANES), lambda i, j: (0, 0))

    def run_trunk(x, mods, mkv, mod_spec_of, tabs, tab_spec_of, tm, mix_a, make_shared, mix_b):
        cos, sin = tabs
        a_states = []
        shared = None
        shared_state = None
        for layer in range(depth):
            md = mods[layer]
            ms = mod_spec_of(tm)
            if layer == n_a:
                rows = _proj(x, mkv[0], mkv[1], ms, g_kv[None], w_kv_bb[:, :4 * gw], cos, sin, tab_spec_of(tm),
                             tm=tm, tn=256, mode="rule", rope_rule=(4, (2,)), name="nsa_kv_rows")
                winr = _proj(x, mkv[0], mkv[1], ms, g_kv[None], w_kv_bb[:, 4 * gw:], cos, sin, tab_spec_of(tm),
                             tm=tm, tn=256, mode="rule", rope_rule=(2, (0,)), name="nsa_kv_win")
                shared, shared_state = make_shared(rows, winr)
            gn = g_norm[layer, 0][None]
            if layer < n_a:
                qkv = [_proj(x, md[0], md[1], ms, gn, w_qkv_b[layer][:, 3 * g * d:3 * (g + 1) * d], cos, sin,
                             tab_spec_of(tm), tm=tm, tn=512, mode="rule", rope_rule=(12, tuple(range(8))),
                             name=f"qkv_a_g{g}") for g in range(len(A_PATTERNS))]
                acts, st = mix_a(layer, qkv)
                a_states.append(st)
                tmo = min(tm, 256)
                x = _oproj(acts, w_o_a_b[layer], x, md[2], mod_spec_of(tmo), tm=tmo, n_grp=3, name="oproj_a")
            else:
                lb = layer - n_a
                q, qr = _proj(x, md[0], md[1], ms, gn, w_q_b[lb], cos, sin, tab_spec_of(tm),
                              tm=tm, tn=512, mode="both", name="q_b")
                gates = _proj(x, md[0], md[1], ms, gn, w_g_b[lb], dummy_tab, dummy_tab, dummy_spec,
                              tm=tm, tn=LANES, mode="sigmoid", name="gates_b")
                o = mix_b(q, qr, gates, shared)
                tmo = min(tm, 512)
                x = _oproj([o], w_o_b_b[lb], x, md[2], mod_spec_of(tmo), tm=tmo, n_grp=0, name="oproj_b")
            tmf = min(tm, 512)
            x = _ffn(x, md[3], md[4], md[5], mod_spec_of(tmf), g_norm[layer, 1][None], w_in_b[layer],
                     w_out_b[layer], tm=tmf, tf=w_out_b.shape[1] // 2, name="ffn")
        y = _final_norm(x, g_final[None], min(tm, 512))
        return y, a_states, shared_state

    tm_p = 1024
    mods_p = [[mod_all[l, :bsz, k][:, None, :] for k in range(6)] for l in range(depth)]
    mkv_p = [mkv_all[:bsz, k][:, None, :] for k in range(2)]
    tabs_p = _rope_tables(jnp.arange(seq))

    def mod_spec_p(tm):
        return _prompt_mod_spec(seq // tm)

    def tab_spec_p(tm):
        tpb = seq // tm
        return pl.BlockSpec((tm, LANES), lambda i, j: (i % tpb, 0))

    def mix_a_p(layer, qkv):
        q3 = [a.reshape(bsz, seq, -1) for a in qkv]
        outs, lses = [], []
        for grp in range(len(A_PATTERNS)):
            o, l = _dil_prompt(q3[grp], grp, bsz, seq)
            outs.append(o)
            lses.append(l)
        st = tuple(q3[g][:, seq - min(win, seq):, d:].reshape(bsz, min(win, seq), 2, N_HEADS, HEAD_DIM)
                   for g, (win, _) in enumerate(A_PATTERNS))
        return outs + lses, st

    n_c_p = (seq - CMP_LEN) // CMP_STRIDE + 1
    n_s_p = seq // SLC_LEN
    cover_p = _cover_matrix(seq // CMP_STRIDE, n_c_p, n_s_p)

    def shared_p(rows, winr):
        table = jnp.arange(bsz * (seq // PAGE), dtype=jnp.int32)
        kc, vc = _compress(rows.reshape(bsz * (seq // PAGE), PAGE, 4 * gw), table, bsz, seq // PAGE, cw)
        r5 = rows.reshape(bsz, seq, 4, N_KV_B, HEAD_DIM)
        blk1h = (jnp.arange(seq)[:, None] // SLC_LEN == jnp.arange(n_s_p)[None, :]).astype(BF16)
        ks = r5[:, :, 2].transpose(0, 2, 1, 3).astype(BF16)
        ka = jnp.concatenate([ks, jnp.broadcast_to(blk1h[None, None], (bsz, N_KV_B, seq, n_s_p))], axis=-1)
        pair = lambda a: a.reshape(bsz, a.shape[1], N_KV_B // 2, 2 * HEAD_DIM).transpose(0, 2, 1, 3).astype(BF16)

        def with_ones(a):
            v = a.reshape(bsz, a.shape[1], N_KV_B, HEAD_DIM).transpose(0, 2, 1, 3).astype(BF16)
            return jnp.concatenate([v, jnp.ones_like(v)], axis=-1)

        vs = with_ones(rows[:, 3 * gw:].reshape(bsz, seq, gw))
        w4 = winr.reshape(bsz, seq, 2, N_KV_B, HEAD_DIM)
        wpad = jnp.pad(winr.reshape(bsz, seq, 2 * gw), ((0, 0), (WIN_B, 0), (0, 0)))
        kw = pair(wpad[:, :, :gw])
        vw = with_ones(wpad[:, :, gw:])
        state = (r5, w4[:, seq - min(WIN_B, seq):])
        return (kc, vc, ka, vs, kw, vw), state

    def mix_b_p(q, qr, gates, shared):
        kc, vc, ka, vs, kw, vw = shared
        oc, sb = _cmp_prompt(q, gates, kc, vc, cover_p, e_c, bsz, seq)
        o = _selwin_prompt(qr, sb, gates, oc, ka, vs, kw, vw, e_s, e_w, bsz, seq)
        return o.reshape(bsz * seq, d)

    y_p, a_p, (rows_p, win_p) = run_trunk(x_prompt.reshape(bsz * seq, d), mods_p, mkv_p, mod_spec_p, tabs_p,
                                          tab_spec_p, tm_p, mix_a_p, shared_p, mix_b_p)

    m_s = nb * t_new
    tm_s = min(m_s, 512)
    mods_s = [[jnp.repeat(mod_all[l, bsz:bsz + nb, k], t_new, axis=0) for k in range(6)] for l in range(depth)]
    mkv_s = [jnp.repeat(mkv_all[bsz:bsz + nb, k], t_new, axis=0) for k in range(2)]
    cos_s, sin_s = _rope_tables(past + jnp.arange(t_new))
    tabs_s = (jnp.tile(cos_s, (nb, 1)), jnp.tile(sin_s, (nb, 1)))
    caches = [c.transpose(0, 1, 3, 4, 5, 2).reshape(c.shape[0], nb, 2 * d, c.shape[2])
              for c in (cache_a0, cache_a1, cache_a2)]
    a_state_s = [None] * len(A_PATTERNS)

    def tab_spec_s(tm):
        return pl.BlockSpec((tm, LANES), lambda i, j: (i, 0))

    def mix_a_s(layer, qkv):
        outs, lses = [], []
        for grp in range(len(A_PATTERNS)):
            o, l, a_state_s[grp] = _dil_sample(qkv[grp], caches[grp], a_state_s[grp], layer, grp, nb, t_new)
            outs.append(o)
            lses.append(l)
        return outs + lses, None

    n_c_s = (past + t_new - CMP_LEN) // CMP_STRIDE + 1
    n_s_s = -(-(past + t_new) // SLC_LEN)
    cover_s = _cover_matrix(past // CMP_STRIDE, n_c_s, n_s_s)
    fold = _fold_matrix()
    blk1h_s = (jnp.arange(LANES)[:, None] == jnp.arange(past + LANES)[None, :] // SLC_LEN).astype(BF16)
    table_s = page_table.reshape(-1).astype(jnp.int32)
    pool3 = cache_b_pool.transpose(0, 2, 3, 4, 1).reshape(cache_b_pool.shape[0], 4 * gw, PAGE)
    wcache = cache_b_win.reshape(nb, cache_b_win.shape[1], 2 * gw)

    def shared_s(rows, winr):
        kc, vc = _compress(pool3, table_s, nb, npages, cw, pages_t=True)
        w4 = winr.reshape(nb, t_new, 2, N_KV_B, HEAD_DIM)
        win_full = jnp.concatenate([cache_b_win, w4], axis=1)
        n_keep = min(WIN_B, win_full.shape[1])
        state = (rows.reshape(nb, t_new, 4, N_KV_B, HEAD_DIM), win_full[:, win_full.shape[1] - n_keep:])
        return (kc, vc, rows, winr), state

    def mix_b_s(q, qr, gates, shared):
        kc, vc, rows, winr = shared
        return _nsa_sample(q, qr, gates, kc, vc, rows, wcache, winr, pool3, table_s,
                           (fold, fold.T, cover_s, blk1h_s), nb, t_new, npages)

    y_s, a_s, (rows_s, win_s) = run_trunk(x_sample.reshape(m_s, d), mods_s, mkv_s, _row_mod_spec, tabs_s,
                                          tab_spec_s, tm_s, mix_a_s, shared_s, mix_b_s)

    outs = [y_p.reshape(bsz, seq, d), y_s.reshape(nb, t_new, d)]
    for g, cache in enumerate((cache_a0, cache_a1, cache_a2)):
        outs.append(jnp.stack([st[g] for st in a_p]))
        st_t = a_state_s[g].reshape(cache.shape[0], nb, 2, N_HEADS, HEAD_DIM, cache.shape[2])
        outs.append(st_t.transpose(0, 1, 5, 2, 3, 4))
    outs += [rows_p, rows_s, win_p, win_s]
    return tuple(outs)
```

```python
import functools

import jax
import jax.numpy as jnp
from jax import lax
from jax.experimental import pallas as pl
from jax.experimental.pallas import tpu as pltpu

F32 = jnp.float32
BF16 = jnp.bfloat16

D_MODEL = 1024
HEAD_DIM = 64
N_HEADS = 16
A_PATTERNS = ((128, 1), (512, 4), (2048, 16))
N_KV_B = 4
HPG_B = 4
CMP_LEN = 32
CMP_STRIDE = 16
SLC_LEN = 64
N_SELECT = 16
WIN_B = 512
PAGE = 128
FORCE_SCORE = 1.0e4
ROPE_THETA = 10000.0
EPS = 1e-6
NEG = -1e30
Q_SCALE = HEAD_DIM ** -0.5
LANES = 128


def _cparams(sem, vmem_mb=48):
    return pltpu.CompilerParams(dimension_semantics=sem, vmem_limit_bytes=vmem_mb << 20)


def _dot(a, b):
    return jnp.dot(a, b, preferred_element_type=F32)


def _dot_nt(a, b):
    return lax.dot_general(a, b, (((1,), (1,)), ((), ())), preferred_element_type=F32)


def _split3(x):
    hi = x.astype(BF16)
    r1 = x - hi.astype(F32)
    mid = r1.astype(BF16)
    lo = (r1 - mid.astype(F32)).astype(BF16)
    return hi, mid, lo


def _dot3(x, m01):
    hi, mid, lo = _split3(x)
    return _dot(hi, m01) + _dot(mid, m01) + _dot(lo, m01)


def _norm_mod(x, g, sh, sc):
    ms = jnp.mean(x * x, axis=-1, keepdims=True)
    y = x * lax.rsqrt(ms + EPS) * g
    return y * (1.0 + sc) + sh


def _rope_tile(a, cos, sin):
    tn = a.shape[1]
    reps = tn // LANES
    if reps > 1:
        cos = jnp.concatenate([cos] * reps, axis=1)
        sin = jnp.concatenate([sin] * reps, axis=1)
    lane = lax.broadcasted_iota(jnp.int32, a.shape, 1)
    first = (lane % HEAD_DIM) < (HEAD_DIM // 2)
    rot = jnp.where(first, pltpu.roll(a, tn - HEAD_DIM // 2, 1), pltpu.roll(a, HEAD_DIM // 2, 1))
    return a * cos + rot * sin


def _rope_tables(pos):
    half = HEAD_DIM // 2
    inv_freq = ROPE_THETA ** (-jnp.arange(half, dtype=F32) / half)
    ang = pos.astype(F32)[:, None] * inv_freq[None, :]
    c, s = jnp.cos(ang), jnp.sin(ang)
    cos = jnp.tile(jnp.concatenate([c, c], axis=1), (1, LANES // HEAD_DIM))
    sin = jnp.tile(jnp.concatenate([-s, s], axis=1), (1, LANES // HEAD_DIM))
    return cos, sin


def _ada_kernel(c_ref, w_ref, b_ref, o_ref):
    c = c_ref[...]
    s = (c * jax.nn.sigmoid(c)).astype(BF16)
    o_ref[...] = _dot(s, w_ref[...]) + b_ref[...]


def _adaln(c, w, b, tn):
    mp, d = c.shape
    n_l, _, n = w.shape
    return pl.pallas_call(
        _ada_kernel,
        grid=(n_l, n // tn),
        in_specs=[pl.BlockSpec((mp, d), lambda l, j: (0, 0)),
                  pl.BlockSpec((None, d, tn), lambda l, j: (l, 0, j)),
                  pl.BlockSpec((None, 1, tn), lambda l, j: (l, 0, j))],
        out_specs=pl.BlockSpec((None, mp, tn), lambda l, j: (l, 0, j)),
        out_shape=jax.ShapeDtypeStruct((n_l, mp, n), F32),
        compiler_params=_cparams(("arbitrary", "arbitrary")),
        name="adaln",
    )(c, w, b)


def _proj_kernel(x_ref, sh_ref, sc_ref, g_ref, w_ref, cos_ref, sin_ref, *rest, mode, tn, rope_rule):
    n_out = 2 if mode in ("both", "rule_bf") else 1
    out_refs = rest[:n_out]
    h_scr = rest[n_out]
    j = pl.program_id(1)

    @pl.when(j == 0)
    def _():
        h_scr[...] = _norm_mod(x_ref[...], g_ref[...], sh_ref[...], sc_ref[...]).astype(BF16)

    acc = _dot(h_scr[...], w_ref[...])
    if mode == "sigmoid":
        out_refs[0][...] = jax.nn.sigmoid(acc)
    elif mode == "both":
        out_refs[0][...] = acc
        out_refs[1][...] = _rope_tile(acc, cos_ref[...], sin_ref[...])
    else:
        period, units = rope_rule
        unit = ((j * tn) // 256) % period
        flag = unit == units[0]
        for u in units[1:]:
            flag = jnp.logical_or(flag, unit == u)

        res = jnp.where(flag, _rope_tile(acc, cos_ref[...], sin_ref[...]), acc)
        out_refs[0][...] = res
        if mode == "rule_bf":
            out_refs[1][...] = res.astype(BF16)


def _proj(x, sh, sc, mod_spec, g, w, cos, sin, tab_spec, *, tm, tn, mode, rope_rule=None, name):
    m, k = x.shape
    n = w.shape[1]
    n_out = 2 if mode in ("both", "rule_bf") else 1
    out_dtypes = [F32, BF16] if mode == "rule_bf" else [F32] * n_out
    kern = functools.partial(_proj_kernel, mode=mode, tn=tn, rope_rule=rope_rule)
    out_spec = pl.BlockSpec((tm, tn), lambda i, j: (i, j))
    out = pl.pallas_call(
        kern,
        grid=(m // tm, n // tn),
        in_specs=[pl.BlockSpec((tm, k), lambda i, j: (i, 0)), mod_spec, mod_spec,
                  pl.BlockSpec((1, k), lambda i, j: (0, 0)),
                  pl.BlockSpec((k, tn), lambda i, j: (0, j)), tab_spec, tab_spec],
        out_specs=[out_spec] * n_out,
        out_shape=[jax.ShapeDtypeStruct((m, n), dt) for dt in out_dtypes],
        scratch_shapes=[pltpu.VMEM((tm, k), BF16)],
        compiler_params=_cparams(("arbitrary", "arbitrary")),
        name=name,
    )(x, sh, sc, g, w, cos, sin)
    return out if n_out == 2 else out[0]


def _oproj_kernel(*refs, n_grp):
    if n_grp:
        o_refs, l_refs = refs[:n_grp], refs[n_grp:2 * n_grp]
        ls = [r[...] for r in l_refs]
        mx = ls[0]
        for l in ls[1:]:
            mx = jnp.maximum(mx, l)
        es = [jnp.exp(l - mx) for l in ls]
        den = es[0]
        num = es[0] * o_refs[0][...]
        for e, o in zip(es[1:], o_refs[1:]):
            den = den + e
            num = num + e * o[...]
        a = num / den
        rest = refs[2 * n_grp:]
    else:
        a = refs[0][...]
        rest = refs[1:]
    w_ref, x_ref, gt_ref, out_ref = rest
    out_ref[...] = x_ref[...] + gt_ref[...] * _dot(a.astype(BF16), w_ref[...])


def _oproj(acts, w, x, gate, mod_spec, *, tm, n_grp, name):
    m, d = x.shape
    k = w.shape[0]
    row = pl.BlockSpec((tm, k), lambda i: (i, 0))
    return pl.pallas_call(
        functools.partial(_oproj_kernel, n_grp=n_grp),
        grid=(m // tm,),
        in_specs=[row] * len(acts) + [pl.BlockSpec((k, d), lambda i: (0, 0)),
                                      pl.BlockSpec((tm, d), lambda i: (i, 0)), mod_spec],
        out_specs=pl.BlockSpec((tm, d), lambda i: (i, 0)),
        out_shape=jax.ShapeDtypeStruct((m, d), F32),
        compiler_params=_cparams(("arbitrary",)),
        name=name,
    )(*acts, w, x, gate)


def _ffn_kernel(x_ref, sh_ref, sc_ref, gt_ref, g_ref, wa_ref, wg_ref, wo_ref, out_ref, h_scr, acc_scr):
    f = pl.program_id(1)

    @pl.when(f == 0)
    def _():
        h_scr[...] = _norm_mod(x_ref[...], g_ref[...], sh_ref[...], sc_ref[...]).astype(BF16)
        acc_scr[...] = jnp.zeros_like(acc_scr)

    h = h_scr[...]
    a = _dot(h, wa_ref[...])
    gg = _dot(h, wg_ref[...])
    act = (a * jax.nn.sigmoid(a)) * gg
    acc_scr[...] += _dot(act.astype(BF16), wo_ref[...])

    @pl.when(f == pl.num_programs(1) - 1)
    def _():
        out_ref[...] = x_ref[...] + gt_ref[...] * acc_scr[...]


def _ffn(x, sh, sc, gt, mod_spec, g, w_in, w_out, *, tm, tf, name):
    m, d = x.shape
    dff = w_out.shape[0]
    nf = dff // tf
    return pl.pallas_call(
        _ffn_kernel,
        grid=(m // tm, nf),
        in_specs=[pl.BlockSpec((tm, d), lambda i, f: (i, 0)), mod_spec, mod_spec, mod_spec,
                  pl.BlockSpec((1, d), lambda i, f: (0, 0)),
                  pl.BlockSpec((d, tf), lambda i, f: (0, f)),
                  pl.BlockSpec((d, tf), lambda i, f: (0, f + nf)),
                  pl.BlockSpec((tf, d), lambda i, f: (f, 0))],
        out_specs=pl.BlockSpec((tm, d), lambda i, f: (i, 0)),
        out_shape=jax.ShapeDtypeStruct((m, d), F32),
        scratch_shapes=[pltpu.VMEM((tm, d), BF16), pltpu.VMEM((tm, d), F32)],
        compiler_params=_cparams(("arbitrary", "arbitrary")),
        name=name,
    )(x, sh, sc, gt, g, w_in, w_in, w_out)


def _final_norm_kernel(x_ref, g_ref, o_ref):
    x = x_ref[...]
    ms = jnp.mean(x * x, axis=-1, keepdims=True)
    o_ref[...] = x * lax.rsqrt(ms + EPS) * g_ref[...]


def _final_norm(x, g, tm):
    m, d = x.shape
    return pl.pallas_call(
        _final_norm_kernel,
        grid=(m // tm,),
        in_specs=[pl.BlockSpec((tm, d), lambda i: (i, 0)), pl.BlockSpec((1, d), lambda i: (0, 0))],
        out_specs=pl.BlockSpec((tm, d), lambda i: (i, 0)),
        out_shape=jax.ShapeDtypeStruct((m, d), F32),
        compiler_params=_cparams(("arbitrary",)),
        name="final_norm",
    )(x, g)


A_BLK = 128


def _dil_prompt_kernel(q_ref, kp_ref, kc_ref, vp_ref, vc_ref, o_ref, l_ref, s_scr, p_scr, den_scr):
    i = pl.program_id(2)
    row = lax.broadcasted_iota(jnp.int32, (A_BLK, A_BLK), 0)
    col = lax.broadcasted_iota(jnp.int32, (A_BLK, A_BLK), 1)
    m_prev = jnp.logical_and(col >= row, i > 0)
    m_cur = col <= row
    heads = [slice(h * HEAD_DIM, (h + 1) * HEAD_DIM) for h in range(N_HEADS)]
    for h, sl in enumerate(heads):
        q = (q_ref[:, sl] * Q_SCALE).astype(BF16)
        s_scr[h, :, :A_BLK] = _dot_nt(q, kp_ref[:, sl].astype(BF16))
        s_scr[h, :, A_BLK:] = _dot_nt(q, kc_ref[:, sl].astype(BF16))
    for h, sl in enumerate(heads):
        s_p = jnp.where(m_prev, s_scr[h, :, :A_BLK], NEG)
        s_c = jnp.where(m_cur, s_scr[h, :, A_BLK:], NEG)
        mx = jnp.max(jnp.maximum(s_p, s_c), axis=-1, keepdims=True)
        p_p = jnp.exp(s_p - mx)
        p_c = jnp.exp(s_c - mx)
        den = jnp.sum(p_p + p_c, axis=-1, keepdims=True)
        p_scr[h, :, :A_BLK] = p_p.astype(BF16)
        p_scr[h, :, A_BLK:] = p_c.astype(BF16)
        den_scr[h] = jnp.broadcast_to(den, (A_BLK, HEAD_DIM))
        l_ref[:, sl] = jnp.broadcast_to(mx + jnp.log(den), (A_BLK, HEAD_DIM))
    for h, sl in enumerate(heads):
        o = (_dot(p_scr[h, :, :A_BLK], vp_ref[:, sl].astype(BF16))
             + _dot(p_scr[h, :, A_BLK:], vc_ref[:, sl].astype(BF16)))
        o_ref[:, sl] = o / den_scr[h]


def _dil_prompt(qkv, grp, bsz, t):
    _, dil = A_PATTERNS[grp]
    tu = t // dil
    ncol = qkv.shape[-1] // D_MODEL
    qv = qkv.reshape(bsz, tu, dil * qkv.shape[-1])

    def spec(which, prev):
        if prev:
            return pl.BlockSpec((None, A_BLK, D_MODEL), lambda b, r, i: (b, jnp.maximum(i - 1, 0), r * ncol + which))
        return pl.BlockSpec((None, A_BLK, D_MODEL), lambda b, r, i: (b, i, r * ncol + which))

    out_spec = pl.BlockSpec((None, A_BLK, D_MODEL), lambda b, r, i: (b, i, r))
    shp = jax.ShapeDtypeStruct((bsz, tu, dil * D_MODEL), F32)
    o, l = pl.pallas_call(
        _dil_prompt_kernel,
        grid=(bsz, dil, tu // A_BLK),
        in_specs=[spec(0, False), spec(1, True), spec(1, False), spec(2, True), spec(2, False)],
        out_specs=[out_spec, out_spec],
        out_shape=[shp, shp],
        scratch_shapes=[pltpu.VMEM((N_HEADS, A_BLK, 2 * A_BLK), F32), pltpu.VMEM((N_HEADS, A_BLK, 2 * A_BLK), BF16),
                        pltpu.VMEM((N_HEADS, A_BLK, HEAD_DIM), F32)],
        compiler_params=_cparams(("arbitrary", "arbitrary", "arbitrary")),
        name=f"dil_prompt_g{grp}",
    )(qv, qv, qv, qv, qv)
    return o.reshape(bsz * t, D_MODEL), l.reshape(bsz * t, D_MODEL)


def _head_mask(rows, cols, row_head, col_div):
    r = lax.broadcasted_iota(jnp.int32, (rows, cols), 0)
    c = lax.broadcasted_iota(jnp.int32, (rows, cols), 1)
    return row_head(r) == c // col_div


def _dil_sample_kernel(q_ref, kn_ref, vn_ref, cur_ref, nxt_ref, *rest, dil, wc, t_new, aliased):
    if aliased:
        rest = rest[1:]
    o_ref, l_ref, st_ref, qx_scr, new_scr, m_scr, l_scr, acc_scr = rest
    c = pl.program_id(1)
    last = c == pl.num_programs(1) - 1
    nrow = N_HEADS * t_new
    hmask = _head_mask(nrow, D_MODEL, lambda r: r // t_new, HEAD_DIM)

    @pl.when(c == 0)
    def _():
        q = q_ref[...] * Q_SCALE
        qt = jnp.broadcast_to(q[None], (N_HEADS, t_new, D_MODEL)).reshape(nrow, D_MODEL)
        qx_scr[...] = jnp.where(hmask, qt, 0.0).astype(BF16)
        pad = jnp.zeros((LANES - t_new, D_MODEL), F32)
        new_scr[:D_MODEL, :] = jnp.concatenate([kn_ref[...], pad], axis=0).T
        new_scr[D_MODEL:, :] = jnp.concatenate([vn_ref[...], pad], axis=0).T
        m_scr[...] = jnp.full_like(m_scr, NEG)
        l_scr[...] = jnp.zeros_like(l_scr)
        acc_scr[...] = jnp.zeros_like(acc_scr)

    qx = qx_scr[...]

    def update(s, v_t):
        m_old = m_scr[...]
        m_new = jnp.maximum(m_old, jnp.max(s, axis=-1, keepdims=True))
        alpha = jnp.exp(m_old - m_new)
        p = jnp.exp(s - m_new)
        l_scr[...] = alpha * l_scr[...] + jnp.sum(p, axis=-1, keepdims=True)
        acc_scr[...] = alpha * acc_scr[...] + _dot_nt(p.astype(BF16), v_t)
        m_scr[...] = m_new

    cur = cur_ref[...]
    qi = lax.broadcasted_iota(jnp.int32, (nrow, wc), 0) % t_new
    rho = c * wc + lax.broadcasted_iota(jnp.int32, (nrow, wc), 1)
    valid = jnp.logical_and((rho - qi) % dil == 0, rho >= qi)
    update(jnp.where(valid, _dot(qx, cur[:D_MODEL].astype(BF16)), NEG), cur[D_MODEL:].astype(BF16))

    rolled = pltpu.roll(cur, wc - t_new, 1)
    tail = pltpu.roll(jnp.where(last, new_scr[...], nxt_ref[...]), LANES - t_new, 1)
    lane = lax.broadcasted_iota(jnp.int32, (2 * D_MODEL, LANES), 1)
    if wc > LANES:
        st_ref[:, :wc - LANES] = rolled[:, :wc - LANES]
    st_ref[:, wc - LANES:] = jnp.where(lane < LANES - t_new, rolled[:, wc - LANES:], tail)

    @pl.when(last)
    def _():
        qi2 = lax.broadcasted_iota(jnp.int32, (nrow, LANES), 0) % t_new
        kj = lax.broadcasted_iota(jnp.int32, (nrow, LANES), 1)
        ok = jnp.logical_and(kj <= qi2, (qi2 - kj) % dil == 0)
        update(jnp.where(ok, _dot(qx, new_scr[:D_MODEL, :].astype(BF16)), NEG), new_scr[D_MODEL:, :].astype(BF16))
        den = l_scr[...]
        res = jnp.where(hmask, acc_scr[...] / den, 0.0)
        lse = jnp.where(hmask, m_scr[...] + jnp.log(den), 0.0)
        o_ref[...] = jnp.sum(res.reshape(N_HEADS, t_new, D_MODEL), axis=0)
        l_ref[...] = jnp.sum(lse.reshape(N_HEADS, t_new, D_MODEL), axis=0)


def _dil_sample(qkv, cache_t, state, layer, grp, nb, t_new):
    win, dil = A_PATTERNS[grp]
    wc = min(win, 512)
    n_chunks = win // wc
    qv = qkv.reshape(nb, t_new, qkv.shape[-1])
    nrow = N_HEADS * t_new
    lanes_per_chunk = wc // LANES

    def new_spec(which):
        return pl.BlockSpec((None, t_new, D_MODEL), lambda b, c: (b, 0, which))

    def nxt_map(b, c):
        return (layer, b, 0, jnp.minimum((c + 1) * lanes_per_chunk, win // LANES - 1))

    out_spec = pl.BlockSpec((None, t_new, D_MODEL), lambda b, c: (b, 0, 0))
    shp = jax.ShapeDtypeStruct((nb, t_new, D_MODEL), F32)
    in_specs = [new_spec(0), new_spec(1), new_spec(2),
                pl.BlockSpec((None, None, 2 * D_MODEL, wc), lambda b, c: (layer, b, 0, c)),
                pl.BlockSpec((None, None, 2 * D_MODEL, LANES), nxt_map)]
    args = [qv, qv, qv, cache_t, cache_t]
    aliases = {}
    if state is not None:
        in_specs.append(pl.BlockSpec(memory_space=pl.ANY))
        args.append(state)
        aliases = {len(args) - 1: 2}
    o, l, st = pl.pallas_call(
        functools.partial(_dil_sample_kernel, dil=dil, wc=wc, t_new=t_new, aliased=state is not None),
        grid=(nb, n_chunks),
        in_specs=in_specs,
        out_specs=[out_spec, out_spec,
                   pl.BlockSpec((None, None, 2 * D_MODEL, wc), lambda b, c: (layer, b, 0, c))],
        out_shape=[shp, shp, jax.ShapeDtypeStruct(cache_t.shape, F32)],
        scratch_shapes=[pltpu.VMEM((nrow, D_MODEL), BF16), pltpu.VMEM((2 * D_MODEL, LANES), F32),
                        pltpu.VMEM((nrow, 1), F32), pltpu.VMEM((nrow, 1), F32), pltpu.VMEM((nrow, D_MODEL), F32)],
        input_output_aliases=aliases,
        compiler_params=_cparams(("arbitrary", "arbitrary"), 56),
        name=f"dil_sample_g{grp}",
    )(*args)
    return o.reshape(nb * t_new, D_MODEL), l.reshape(nb * t_new, D_MODEL), st


CMP_SUB = PAGE // CMP_STRIDE


CMP_PAGES = 4
CMP_CHUNKS = 4


def _compress_kernel(*refs, pages_t):
    n_pg = CMP_PAGES * CMP_CHUNKS
    page_refs = refs[1:1 + n_pg]
    pelo_ref, pehi_ref, w1lo_ref, w1hi_ref, b1_ref, w2_ref, b2_ref, kc_ref, vc_ref, zlo, zhi = refs[1 + n_pg:][:11]
    k = pl.program_id(1)
    nsteps = pl.num_programs(1)
    nu = zlo.shape[1]
    for pp in range(CMP_PAGES):
        r0 = pl.multiple_of((k * CMP_PAGES + pp) * CMP_SUB, CMP_SUB)
        for m in range(CMP_CHUNKS):
            page_ref = page_refs[pp * CMP_CHUNKS + m]
            if pages_t:
                tr_scr = refs[-1]
                tr_scr[pp * CMP_CHUNKS + m] = page_ref[...].T
                page_ref = tr_scr.at[pp * CMP_CHUNKS + m]
            for p in range(CMP_STRIDE):
                rows = page_ref[pl.ds(p, CMP_SUB, stride=CMP_STRIDE), :]
                lanes = slice(p * LANES, (p + 1) * LANES)
                zlo[m, pl.ds(r0, CMP_SUB), lanes] = rows + pelo_ref[p:p + 1, m * LANES:(m + 1) * LANES]
                zhi[m, pl.ds(r0, CMP_SUB), lanes] = rows + pehi_ref[p:p + 1, m * LANES:(m + 1) * LANES]

    @pl.when(k == nsteps - 1)
    def _():
        for m in range(4):
            ty = m // 2
            first = _dot(zlo[m].astype(BF16), w1lo_ref[ty])
            second = _dot(zhi[m].astype(BF16), w1hi_ref[ty])
            hid = first + pltpu.roll(second, nu - 1, 0) + b1_ref[ty]
            out = _dot(jax.nn.gelu(hid).astype(BF16), w2_ref[ty]) + b2_ref[ty]
            dst = kc_ref if ty == 0 else vc_ref
            dst[:, (m % 2) * LANES:(m % 2 + 1) * LANES] = out


def _compress(pages, table, nb, npages, cw, pages_t=False):
    pelo, pehi, w1lo, w1hi, b1, w2, b2 = cw
    nu = npages * CMP_SUB
    const2 = lambda shape: pl.BlockSpec(shape, lambda b, k, pt: (0,) * len(shape))

    def chunk_spec(pp, m):
        page = lambda b, k, pt: pt[b * npages + k * CMP_PAGES + pp]
        if pages_t:
            return pl.BlockSpec((None, LANES, PAGE), lambda b, k, pt: (page(b, k, pt), m, 0))
        return pl.BlockSpec((None, PAGE, LANES), lambda b, k, pt: (page(b, k, pt), 0, m))

    n_pg = CMP_PAGES * CMP_CHUNKS
    tr_scratch = [pltpu.VMEM((n_pg, PAGE, LANES), F32)] if pages_t else []
    grid_spec = pltpu.PrefetchScalarGridSpec(
        num_scalar_prefetch=1,
        grid=(nb, npages // CMP_PAGES),
        in_specs=[chunk_spec(pp, m) for pp in range(CMP_PAGES) for m in range(CMP_CHUNKS)]
                 + [const2(pelo.shape), const2(pehi.shape), const2(w1lo.shape), const2(w1hi.shape),
                    const2(b1.shape), const2(w2.shape), const2(b2.shape)],
        out_specs=[pl.BlockSpec((None, nu, N_KV_B * HEAD_DIM), lambda b, k, pt: (b, 0, 0))] * 2,
        scratch_shapes=[pltpu.VMEM((4, nu, CMP_STRIDE * LANES), F32)] * 2 + tr_scratch,
    )
    shp = jax.ShapeDtypeStruct((nb, nu, N_KV_B * HEAD_DIM), F32)
    return pl.pallas_call(
        functools.partial(_compress_kernel, pages_t=pages_t), grid_spec=grid_spec, out_shape=[shp, shp],
        compiler_params=_cparams(("arbitrary", "arbitrary")), name="nsa_compress",
    )(table, *([pages] * n_pg), pelo, pehi, w1lo, w1hi, b1, w2, b2)


def _compress_weights(pe_cmp, w_cmp1, b_cmp1, w_cmp2, b_cmp2):
    eye2 = jnp.eye(2, dtype=F32)
    hid = w_cmp1.shape[-1]
    w1 = w_cmp1.reshape(2, CMP_LEN, HEAD_DIM, hid)

    def pair_w1(w):
        return jnp.einsum("pen,ab->paebn", w, eye2).reshape(CMP_STRIDE * LANES, 2 * hid)

    w1lo = jnp.stack([pair_w1(w1[ty, :CMP_STRIDE]) for ty in range(2)]).astype(BF16)
    w1hi = jnp.stack([pair_w1(w1[ty, CMP_STRIDE:]) for ty in range(2)]).astype(BF16)
    w2 = jnp.stack([jnp.einsum("ne,ab->anbe", w_cmp2[ty], eye2).reshape(2 * hid, 2 * HEAD_DIM)
                    for ty in range(2)]).astype(BF16)
    b1 = jnp.tile(b_cmp1, (1, 2))[:, None, :]
    b2 = jnp.tile(b_cmp2, (1, 2))[:, None, :]
    pelo = jnp.concatenate([jnp.tile(pe_cmp[ty, :CMP_STRIDE], (1, N_KV_B)) for ty in range(2)], axis=1)
    pehi = jnp.concatenate([jnp.tile(pe_cmp[ty, CMP_STRIDE:], (1, N_KV_B)) for ty in range(2)], axis=1)
    return pelo, pehi, w1lo, w1hi, b1, w2, b2


C_TQ = 128


def _rank_select(imp_t, n_blk):
    sidx = lax.broadcasted_iota(jnp.int32, imp_t.shape, 0)
    cnt = jnp.zeros(imp_t.shape, jnp.int32)
    for s in range(n_blk):
        row = imp_t[s:s + 1, :]
        ahead = jnp.logical_or(row > imp_t, jnp.logical_and(row == imp_t, sidx > s))
        cnt = cnt + ahead.astype(jnp.int32)
    return cnt < N_SELECT


def _cmp_prompt_kernel(q_ref, gates_ref, kc_ref, vc_ref, cover_ref, ec_ref, oc_ref, sb_ref):
    i = pl.program_id(1)
    n_c = kc_ref.shape[0]
    n_s = sb_ref.shape[-1]
    t = i * C_TQ + lax.broadcasted_iota(jnp.int32, (C_TQ, n_c), 0)
    c_idx = lax.broadcasted_iota(jnp.int32, (C_TQ, n_c), 1)
    valid = c_idx * CMP_STRIDE + (CMP_LEN - 1) <= t
    any_valid = (t[:, :1] >= CMP_LEN - 1).astype(F32)
    tq = i * C_TQ + lax.broadcasted_iota(jnp.int32, (C_TQ, LANES), 0)
    jb = lax.broadcasted_iota(jnp.int32, (C_TQ, LANES), 1)
    cur = tq // SLC_LEN
    forced = jnp.logical_or(jb == 0, jnp.logical_or(jb == cur, jb == cur - 1))
    cover = cover_ref[...]
    gates = gates_ref[...]
    gh = gates.astype(BF16)
    gl = (gates - gh.astype(F32)).astype(BF16)
    gfull = _dot(gh, ec_ref[...]) + _dot(gl, ec_ref[...])
    for g in range(N_KV_B):
        gs = slice(g * HEAD_DIM, (g + 1) * HEAD_DIM)
        kc = kc_ref[:, gs].astype(BF16)
        vc = vc_ref[:, gs].astype(BF16)
        psum = jnp.zeros((C_TQ, n_c), F32)
        for j in range(HPG_B):
            h = g * HPG_B + j
            hs = slice(h * HEAD_DIM, (h + 1) * HEAD_DIM)
            q = (q_ref[:, hs] * Q_SCALE).astype(BF16)
            s = jnp.where(valid, _dot_nt(q, kc), NEG)
            mx = jnp.max(s, axis=-1, keepdims=True)
            e = jnp.exp(s - mx)
            p = e / jnp.sum(e, axis=-1, keepdims=True) * any_valid
            psum = psum + p
            oc_ref[:, hs] = _dot(p.astype(BF16), vc) * gfull[:, hs]
        imp = _dot3(psum, cover)
        imp = jnp.where(forced, FORCE_SCORE, jnp.where(jb <= cur, imp, -1.0))
        sel_t = _rank_select(imp.T[:n_s, :], n_s)
        bias_t = jnp.where(sel_t, 0.0, NEG)
        bias_t = jnp.concatenate([bias_t, jnp.zeros((LANES - n_s, C_TQ), F32)], axis=0)
        sb_ref[g] = bias_t.T[:, :n_s].astype(BF16)


def _cmp_prompt(q, gates, kc, vc, cover, e_c, bsz, t):
    n_c = kc.shape[1]
    n_s = t // SLC_LEN
    qv = q.reshape(bsz, t, D_MODEL)
    gv = gates.reshape(bsz, t, LANES)
    oc, sb = pl.pallas_call(
        _cmp_prompt_kernel,
        grid=(bsz, t // C_TQ),
        in_specs=[pl.BlockSpec((None, C_TQ, D_MODEL), lambda b, i: (b, i, 0)),
                  pl.BlockSpec((None, C_TQ, LANES), lambda b, i: (b, i, 0)),
                  pl.BlockSpec((None, n_c, N_KV_B * HEAD_DIM), lambda b, i: (b, 0, 0)),
                  pl.BlockSpec((None, n_c, N_KV_B * HEAD_DIM), lambda b, i: (b, 0, 0)),
                  pl.BlockSpec(cover.shape, lambda b, i: (0, 0)),
                  pl.BlockSpec(e_c.shape, lambda b, i: (0, 0))],
        out_specs=[pl.BlockSpec((None, C_TQ, D_MODEL), lambda b, i: (b, i, 0)),
                   pl.BlockSpec((None, N_KV_B, C_TQ, n_s), lambda b, i: (b, 0, i, 0))],
        out_shape=[jax.ShapeDtypeStruct((bsz, t, D_MODEL), F32),
                   jax.ShapeDtypeStruct((bsz, N_KV_B, t, n_s), BF16)],
        compiler_params=_cparams(("arbitrary", "arbitrary")),
        name="nsa_cmp_prompt",
    )(qv, gv, kc, vc, cover, e_c)
    return oc, sb


S_TQ = 128
S_TK = 256
W_TK = 128
S_GROUP_SETS = ((0, 1, 2, 3),)


def _selwin_prompt_kernel(qr_ref, sb_ref, gates_ref, oc_ref, ka_ref, vs_ref, kw_ref, vw_ref, es_ref, ew_ref, o_ref,
                          s_scr):
    i = pl.program_id(1)
    t0 = i * S_TQ
    nrow = HPG_B * S_TQ
    qr = qr_ref[...] * Q_SCALE
    tpos = t0 + lax.broadcasted_iota(jnp.int32, (nrow, 1), 0) % S_TQ
    zeros = jnp.zeros((nrow, HEAD_DIM), BF16)
    q_aug, q_win = [], []
    for g in range(N_KV_B):
        heads = [qr[:, (g * HPG_B + j) * HEAD_DIM:(g * HPG_B + j + 1) * HEAD_DIM] for j in range(HPG_B)]
        qa = jnp.concatenate(heads, axis=0).astype(BF16)
        q_aug.append(jnp.concatenate([qa, jnp.concatenate([sb_ref[g]] * HPG_B, axis=0)], axis=1))
        q_win.append(jnp.concatenate([qa, zeros] if g % 2 == 0 else [zeros, qa], axis=1))

    def online(carry, s, v):
        m_old, acc = carry
        m_new = jnp.maximum(m_old, jnp.max(s, axis=-1, keepdims=True))
        p = jnp.exp(s - m_new)
        return m_new, jnp.exp(m_old - m_new) * acc + _dot(p.astype(BF16), v)

    groups = tuple(range(N_KV_B))

    def scores(kt, g):
        return _dot_nt(q_aug[g], ka_ref[g, pl.ds(pl.multiple_of(kt * S_TK, S_TK), S_TK), :])

    def sel_step(kt, carry, diagonal, gset):
        k0 = pl.multiple_of(kt * S_TK, S_TK)
        out = []
        for c, g in zip(carry, gset):
            s = s_scr[g]
            if diagonal:
                kpos = k0 + lax.broadcasted_iota(jnp.int32, (nrow, S_TK), 1)
                s = jnp.where(kpos <= tpos, s, NEG)
            else:
                s_scr[g] = scores(kt + 1, g)
            out.append(online(c, s, vs_ref[g, pl.ds(k0, S_TK), :]))
        return tuple(out)

    n_full = t0 // S_TK
    sel = [None] * N_KV_B
    for gset in S_GROUP_SETS:
        init = tuple((jnp.full((nrow, 1), NEG, F32), jnp.zeros((nrow, 2 * HEAD_DIM), F32)) for _ in gset)
        for g in gset:
            s_scr[g] = scores(0, g)
        part = lax.fori_loop(0, n_full, lambda kt, c, gset=gset: sel_step(kt, c, False, gset), init)
        part = sel_step(n_full, part, True, gset)
        for c, g in zip(part, gset):
            sel[g] = c

    n_win = (WIN_B + S_TQ) // W_TK
    win = []
    for g in groups:
        tiles = []
        for w in range(n_win):
            k0 = pl.multiple_of(t0 + w * W_TK, SLC_LEN)
            kpos = k0 - WIN_B + lax.broadcasted_iota(jnp.int32, (nrow, W_TK), 1)
            ok = jnp.logical_and(kpos >= 0, jnp.logical_and(kpos <= tpos, tpos - kpos <= WIN_B))
            tiles.append(jnp.where(ok, _dot_nt(q_win[g], kw_ref[g // 2, pl.ds(k0, W_TK), :]), NEG))
        mx = jnp.max(tiles[0], axis=-1, keepdims=True)
        for s in tiles[1:]:
            mx = jnp.maximum(mx, jnp.max(s, axis=-1, keepdims=True))
        acc = jnp.zeros((nrow, 2 * HEAD_DIM), F32)
        for w, s in enumerate(tiles):
            k0 = pl.multiple_of(t0 + w * W_TK, SLC_LEN)
            acc = acc + _dot(jnp.exp(s - mx).astype(BF16), vw_ref[g, pl.ds(k0, W_TK), :])
        win.append((mx, acc))

    gates = gates_ref[...]
    gh = gates.astype(BF16)
    gl = (gates - gh.astype(F32)).astype(BF16)
    g_s = _dot(gh, es_ref[...]) + _dot(gl, es_ref[...])
    g_w = _dot(gh, ew_ref[...]) + _dot(gl, ew_ref[...])
    for g in groups:
        o_s = sel[g][1][:, :HEAD_DIM] / sel[g][1][:, HEAD_DIM:HEAD_DIM + 1]
        o_w = win[g][1][:, :HEAD_DIM] / win[g][1][:, HEAD_DIM:HEAD_DIM + 1]
        for j in range(HPG_B):
            hs = slice((g * HPG_B + j) * HEAD_DIM, (g * HPG_B + j + 1) * HEAD_DIM)
            rs = slice(j * S_TQ, (j + 1) * S_TQ)
            o_ref[:, hs] = oc_ref[:, hs] + g_s[:, hs] * o_s[rs, :] + g_w[:, hs] * o_w[rs, :]


def _selwin_prompt(qr, sb, gates, oc, ka, vs, kw, vw, e_s, e_w, bsz, t):
    qv = qr.reshape(bsz, t, D_MODEL)
    gv = gates.reshape(bsz, t, LANES)
    n_s = sb.shape[-1]
    tp = kw.shape[2]
    tok = pl.BlockSpec((None, S_TQ, D_MODEL), lambda b, i: (b, i, 0))
    resident = lambda a: pl.BlockSpec((None,) + a.shape[1:], lambda b, i: (b, 0, 0, 0))
    return pl.pallas_call(
        _selwin_prompt_kernel,
        grid=(bsz, t // S_TQ),
        in_specs=[tok,
                  pl.BlockSpec((None, N_KV_B, S_TQ, n_s), lambda b, i: (b, 0, i, 0)),
                  pl.BlockSpec((None, S_TQ, LANES), lambda b, i: (b, i, 0)),
                  tok, resident(ka), resident(vs), resident(kw), resident(vw),
                  pl.BlockSpec(e_s.shape, lambda b, i: (0, 0)),
                  pl.BlockSpec(e_w.shape, lambda b, i: (0, 0))],
        out_specs=tok,
        out_shape=jax.ShapeDtypeStruct((bsz, t, D_MODEL), F32),
        scratch_shapes=[pltpu.VMEM((N_KV_B, HPG_B * S_TQ, S_TK), F32)],
        compiler_params=_cparams(("arbitrary", "arbitrary"), 56),
        name="nsa_selwin_prompt",
    )(qv, sb, gv, oc, ka, vs, kw, vw, e_s, e_w)


def _nsa_sample_kernel(pt_ref, q_ref, qr_ref, gates_ref, kc_ref, vc_ref, newrows_ref, wcache_ref, wnew_ref,
                       fold_ref, foldt_ref, cover_ref, blk1h_ref, *rest, npages, t_new, n_c_valid, n_s_valid):
    del pt_ref
    page_refs = rest[:npages]
    o_ref = rest[npages]
    nrow = N_HEADS * t_new
    gw = N_KV_B * HEAD_DIM
    row_head = lambda r: ((r // t_new) % N_KV_B) * HPG_B + r // (t_new * N_KV_B)
    hmask = _head_mask(nrow, D_MODEL, row_head, HEAD_DIM)
    fold = fold_ref[...]

    def qexp(ref):
        q = ref[...] * Q_SCALE
        qt = jnp.broadcast_to(q[None], (N_HEADS, t_new, D_MODEL)).reshape(nrow, D_MODEL)
        return _dot(jnp.where(hmask, qt, 0.0).astype(BF16), fold).astype(BF16)

    qc = qexp(q_ref)
    qr = qexp(qr_ref)
    qi = lax.broadcasted_iota(jnp.int32, (nrow, LANES), 0) % t_new
    lane = lax.broadcasted_iota(jnp.int32, (nrow, LANES), 1)
    new_ok = lane <= qi

    def pad_new(x):
        return jnp.concatenate([x, jnp.zeros((LANES - t_new, x.shape[1]), F32)], axis=0).astype(BF16)

    kc = kc_ref[...].astype(BF16)
    s = jnp.where(lane < n_c_valid, _dot_nt(qc, kc), NEG)
    e = jnp.exp(s - jnp.max(s, axis=-1, keepdims=True))
    p_c = e / jnp.sum(e, axis=-1, keepdims=True)
    res_c = _dot(p_c.astype(BF16), vc_ref[...].astype(BF16))

    ng = N_KV_B * t_new
    psum = p_c[0:ng] + p_c[ng:2 * ng] + p_c[2 * ng:3 * ng] + p_c[3 * ng:4 * ng]
    imp = _dot3(psum, cover_ref[...])
    jb = lax.broadcasted_iota(jnp.int32, (ng, LANES), 1)
    cur = n_s_valid - 1
    forced = jnp.logical_or(jb == 0, jnp.logical_or(jb == cur, jb == cur - 1))
    imp = jnp.where(forced, FORCE_SCORE, jnp.where(jb <= cur, imp, -2.0))
    cnt = jnp.zeros((ng, LANES), jnp.int32)
    for sblk in range(n_s_valid):
        colv = jnp.broadcast_to(imp[:, sblk:sblk + 1], (ng, LANES))
        ahead = jnp.logical_or(colv > imp, jnp.logical_and(colv == imp, jb > sblk))
        cnt = cnt + ahead.astype(jnp.int32)
    bias_g = jnp.where(cnt < N_SELECT, 0.0, NEG).astype(BF16)
    bias = jnp.concatenate([bias_g] * HPG_B, axis=0)
    key_bias = _dot(bias, blk1h_ref[...])

    past = npages * PAGE
    s_parts = [_dot(qr, page_refs[k][:gw, :].astype(BF16)) for k in range(npages)]
    s_new = jnp.where(new_ok, _dot_nt(qr, pad_new(newrows_ref[:, :gw])), NEG)
    s_all = jnp.concatenate(s_parts + [s_new], axis=1) + key_bias
    mx = jnp.max(s_all, axis=-1, keepdims=True)
    p_s = jnp.exp(s_all - mx)
    den_s = jnp.sum(p_s, axis=-1, keepdims=True)
    p_sb = p_s.astype(BF16)
    res_s = _dot(p_sb[:, past:], pad_new(newrows_ref[:, gw:]))
    for k in range(npages):
        res_s = res_s + _dot_nt(p_sb[:, k * PAGE:(k + 1) * PAGE], page_refs[k][gw:, :].astype(BF16))
    res_s = res_s / den_s

    nw = wcache_ref.shape[0]
    qiw = lax.broadcasted_iota(jnp.int32, (nrow, nw), 0) % t_new
    rw = lax.broadcasted_iota(jnp.int32, (nrow, nw), 1)
    s_w = jnp.where(rw >= qiw + (nw - WIN_B), _dot_nt(qr, wcache_ref[:, :gw].astype(BF16)), NEG)
    s_wn = jnp.where(new_ok, _dot_nt(qr, pad_new(wnew_ref[:, :gw])), NEG)
    mxw = jnp.maximum(jnp.max(s_w, axis=-1, keepdims=True), jnp.max(s_wn, axis=-1, keepdims=True))
    p_w = jnp.exp(s_w - mxw)
    p_wn = jnp.exp(s_wn - mxw)
    den_w = jnp.sum(p_w, axis=-1, keepdims=True) + jnp.sum(p_wn, axis=-1, keepdims=True)
    res_w = (_dot(p_w.astype(BF16), wcache_ref[:, gw:].astype(BF16))
             + _dot(p_wn.astype(BF16), pad_new(wnew_ref[:, gw:]))) / den_w

    gt = jnp.broadcast_to(gates_ref[...][None], (N_HEADS, t_new, LANES)).reshape(nrow, LANES)
    rh = row_head(lax.broadcasted_iota(jnp.int32, (nrow, LANES), 0))

    def gate(branch):
        return jnp.sum(jnp.where(lane == 3 * rh + branch, gt, 0.0), axis=-1, keepdims=True)

    tot = gate(0) * res_c + gate(1) * res_s + gate(2) * res_w
    wide = jnp.where(hmask, _dot3(tot, foldt_ref[...]), 0.0)
    o_ref[...] = jnp.sum(wide.reshape(N_HEADS, t_new, D_MODEL), axis=0)


def _nsa_sample(q, qr, gates, kc, vc, rows_new, wcache, win_new, pool, table, consts, nb, t_new, npages):
    fold, foldt, cover, blk1h = consts
    gw = N_KV_B * HEAD_DIM
    n_c = kc.shape[1]
    nw = wcache.shape[1]
    tok = lambda w: pl.BlockSpec((None, t_new, w), lambda b, pt: (b, 0, 0))
    const = lambda a: pl.BlockSpec(a.shape, lambda b, pt: (0,) * a.ndim)

    def page_spec(k):
        return pl.BlockSpec((None, 2 * gw, PAGE), lambda b, pt: (pt[b * npages + k], 1, 0))

    grid_spec = pltpu.PrefetchScalarGridSpec(
        num_scalar_prefetch=1,
        grid=(nb,),
        in_specs=[tok(D_MODEL), tok(D_MODEL), tok(LANES),
                  pl.BlockSpec((None, n_c, gw), lambda b, pt: (b, 0, 0)),
                  pl.BlockSpec((None, n_c, gw), lambda b, pt: (b, 0, 0)),
                  pl.BlockSpec((None, t_new, 2 * gw), lambda b, pt: (b, 0, 1)),
                  pl.BlockSpec((None, nw, 2 * gw), lambda b, pt: (b, 0, 0)),
                  tok(2 * gw), const(fold), const(foldt), const(cover), const(blk1h)]
                 + [page_spec(k) for k in range(npages)],
        out_specs=pl.BlockSpec((None, t_new, D_MODEL), lambda b, pt: (b, 0, 0)),
    )
    kern = functools.partial(_nsa_sample_kernel, npages=npages, t_new=t_new,
                             n_c_valid=(npages * PAGE + t_new - CMP_LEN) // CMP_STRIDE + 1,
                             n_s_valid=-(-(npages * PAGE + t_new) // SLC_LEN))
    out = pl.pallas_call(
        kern, grid_spec=grid_spec,
        out_shape=jax.ShapeDtypeStruct((nb, t_new, D_MODEL), F32),
        compiler_params=_cparams(("arbitrary",)), name="nsa_sample",
    )(table, q.reshape(nb, t_new, D_MODEL), qr.reshape(nb, t_new, D_MODEL), gates.reshape(nb, t_new, LANES),
      kc, vc, rows_new.reshape(nb, t_new, 4 * gw), wcache, win_new.reshape(nb, t_new, 2 * gw),
      fold, foldt, cover, blk1h, *([pool] * npages))
    return out.reshape(nb * t_new, D_MODEL)


def _gate_expand(branch):
    r = jnp.arange(LANES)[:, None]
    c = jnp.arange(D_MODEL)[None, :]
    return (r == 3 * (c // HEAD_DIM) + branch).astype(BF16)


def _cover_matrix(n_c_rows, n_c_valid, n_s):
    c = jnp.arange(n_c_rows)[:, None]
    s = jnp.arange(LANES)[None, :]
    c_start = c * CMP_STRIDE
    s_start = s * SLC_LEN
    hit = (c_start < s_start + SLC_LEN) & (c_start + CMP_LEN > s_start) & (c < n_c_valid) & (s < n_s)
    return hit.astype(BF16)


def _fold_matrix():
    r = jnp.arange(D_MODEL)[:, None]
    c = jnp.arange(N_KV_B * HEAD_DIM)[None, :]
    return ((r // (HPG_B * HEAD_DIM) == c // HEAD_DIM) & (r % HEAD_DIM == c % HEAD_DIM)).astype(BF16)


def _prompt_mod_spec(tiles_per_batch):
    return pl.BlockSpec((None, 1, D_MODEL), lambda i, *_: (i // tiles_per_batch, 0, 0))


def _row_mod_spec(tm):
    return pl.BlockSpec((tm, D_MODEL), lambda i, *_: (i, 0))


def kernel(x_prompt, x_sample, c_prompt, c_sample, cache_a0, cache_a1, cache_a2, cache_b_pool, cache_b_win,
           page_table, g_norm, w_ada, b_ada, w_qkv_a, w_o_a, g_kv, w_ada_kv, b_ada_kv, w_kv_b, pe_cmp, w_cmp1,
           b_cmp1, w_cmp2, b_cmp2, w_qg_b, w_o_b, w_ffn_in, w_ffn_out, g_final):
    bsz, seq, d = x_prompt.shape
    nb, t_new, _ = x_sample.shape
    depth = w_ada.shape[0]
    n_a = w_qkv_a.shape[0]
    past = page_table.shape[1] * PAGE
    npages = page_table.shape[1]
    gw = N_KV_B * HEAD_DIM
    hq = N_HEADS * HEAD_DIM

    w_ada_b = w_ada.astype(BF16)
    w_ada_kv_b = w_ada_kv.astype(BF16)[None]
    w_qkv_b = w_qkv_a.astype(BF16)
    w_o_a_b = w_o_a.astype(BF16)
    w_kv_bb = w_kv_b.astype(BF16)
    w_q_b = w_qg_b[:, :, :hq].astype(BF16)
    w_g_b = jnp.pad(w_qg_b[:, :, hq:], ((0, 0), (0, 0), (0, LANES - 3 * N_HEADS))).astype(BF16)
    w_o_b_b = w_o_b.astype(BF16)
    w_in_b = w_ffn_in.astype(BF16)
    w_out_b = w_ffn_out.astype(BF16)
    cw = _compress_weights(pe_cmp, w_cmp1, b_cmp1, w_cmp2, b_cmp2)

    n_c_all = bsz + nb
    mp = -(-n_c_all // 8) * 8
    c_all = jnp.pad(jnp.concatenate([c_prompt, c_sample], axis=0), ((0, mp - n_c_all), (0, 0)))
    mod_all = _adaln(c_all, w_ada_b, b_ada[:, None, :], 6 * d // 4).reshape(depth, mp, 6, d)
    mkv_all = _adaln(c_all, w_ada_kv_b, b_ada_kv[None, None, :], d)[0].reshape(mp, 2, d)

    e_c, e_s, e_w = _gate_expand(0), _gate_expand(1), _gate_expand(2)
    dummy_tab = jnp.zeros((8, LANES), F32)
    dummy_spec = pl.BlockSpec((8, LANES), lambda i, j: (0, 0))

    def run_trunk(x, mods, mkv, mod_spec_of, tabs, tab_spec_of, tm, mix_a, make_shared, mix_b):
        cos, sin = tabs
        a_states = []
        shared = None
        shared_state = None
        for layer in range(depth):
            md = mods[layer]
            ms = mod_spec_of(tm)
            if layer == n_a:
                rows = _proj(x, mkv[0], mkv[1], ms, g_kv[None], w_kv_bb[:, :4 * gw], cos, sin, tab_spec_of(tm),
                             tm=tm, tn=256, mode="rule", rope_rule=(4, (2,)), name="nsa_kv_rows")
                winr = _proj(x, mkv[0], mkv[1], ms, g_kv[None], w_kv_bb[:, 4 * gw:], cos, sin, tab_spec_of(tm),
                             tm=tm, tn=256, mode="rule", rope_rule=(2, (0,)), name="nsa_kv_win")
                shared, shared_state = make_shared(rows, winr)
            gn = g_norm[layer, 0][None]
            if layer < n_a:
                qkv = [_proj(x, md[0], md[1], ms, gn, w_qkv_b[layer][:, 3 * g * d:3 * (g + 1) * d], cos, sin,
                             tab_spec_of(tm), tm=tm, tn=512, mode="rule_bf", rope_rule=(12, tuple(range(8))),
                             name=f"qkv_a_g{g}") for g in range(len(A_PATTERNS))]
                acts, st = mix_a(layer, qkv)
                a_states.append(st)
                tmo = min(tm, 256)
                x = _oproj(acts, w_o_a_b[layer], x, md[2], mod_spec_of(tmo), tm=tmo, n_grp=3, name="oproj_a")
            else:
                lb = layer - n_a
                q, qr = _proj(x, md[0], md[1], ms, gn, w_q_b[lb], cos, sin, tab_spec_of(tm),
                              tm=tm, tn=512, mode="both", name="q_b")
                gates = _proj(x, md[0], md[1], ms, gn, w_g_b[lb], dummy_tab, dummy_tab, dummy_spec,
                              tm=tm, tn=LANES, mode="sigmoid", name="gates_b")
                o = mix_b(q, qr, gates, shared)
                tmo = min(tm, 512)
                x = _oproj([o], w_o_b_b[lb], x, md[2], mod_spec_of(tmo), tm=tmo, n_grp=0, name="oproj_b")
            tmf = min(tm, 512)
            x = _ffn(x, md[3], md[4], md[5], mod_spec_of(tmf), g_norm[layer, 1][None], w_in_b[layer],
                     w_out_b[layer], tm=tmf, tf=w_out_b.shape[1] // 2, name="ffn")
        y = _final_norm(x, g_final[None], min(tm, 512))
        return y, a_states, shared_state

    tm_p = 1024
    mods_p = [[mod_all[l, :bsz, k][:, None, :] for k in range(6)] for l in range(depth)]
    mkv_p = [mkv_all[:bsz, k][:, None, :] for k in range(2)]
    tabs_p = _rope_tables(jnp.arange(seq))

    def mod_spec_p(tm):
        return _prompt_mod_spec(seq // tm)

    def tab_spec_p(tm):
        tpb = seq // tm
        return pl.BlockSpec((tm, LANES), lambda i, j: (i % tpb, 0))

    def mix_a_p(layer, qkv):
        q3 = [a[0].reshape(bsz, seq, -1) for a in qkv]
        outs, lses = [], []
        for grp in range(len(A_PATTERNS)):
            o, l = _dil_prompt(qkv[grp][1].reshape(bsz, seq, -1), grp, bsz, seq)
            outs.append(o)
            lses.append(l)
        st = tuple(q3[g][:, seq - min(win, seq):, d:].reshape(bsz, min(win, seq), 2, N_HEADS, HEAD_DIM)
                   for g, (win, _) in enumerate(A_PATTERNS))
        return outs + lses, st

    n_c_p = (seq - CMP_LEN) // CMP_STRIDE + 1
    n_s_p = seq // SLC_LEN
    cover_p = _cover_matrix(seq // CMP_STRIDE, n_c_p, n_s_p)

    def shared_p(rows, winr):
        table = jnp.arange(bsz * (seq // PAGE), dtype=jnp.int32)
        kc, vc = _compress(rows.reshape(bsz * (seq // PAGE), PAGE, 4 * gw), table, bsz, seq // PAGE, cw)
        r5 = rows.reshape(bsz, seq, 4, N_KV_B, HEAD_DIM)
        blk1h = (jnp.arange(seq)[:, None] // SLC_LEN == jnp.arange(n_s_p)[None, :]).astype(BF16)
        ks = r5[:, :, 2].transpose(0, 2, 1, 3).astype(BF16)
        ka = jnp.concatenate([ks, jnp.broadcast_to(blk1h[None, None], (bsz, N_KV_B, seq, n_s_p))], axis=-1)
        pair = lambda a: a.reshape(bsz, a.shape[1], N_KV_B // 2, 2 * HEAD_DIM).transpose(0, 2, 1, 3).astype(BF16)

        def with_ones(a):
            v = a.reshape(bsz, a.shape[1], N_KV_B, HEAD_DIM).transpose(0, 2, 1, 3).astype(BF16)
            return jnp.concatenate([v, jnp.ones_like(v)], axis=-1)

        vs = with_ones(rows[:, 3 * gw:].reshape(bsz, seq, gw))
        w4 = winr.reshape(bsz, seq, 2, N_KV_B, HEAD_DIM)
        wpad = jnp.pad(winr.reshape(bsz, seq, 2 * gw), ((0, 0), (WIN_B, 0), (0, 0)))
        kw = pair(wpad[:, :, :gw])
        vw = with_ones(wpad[:, :, gw:])
        state = (r5, w4[:, seq - min(WIN_B, seq):])
        return (kc, vc, ka, vs, kw, vw), state

    def mix_b_p(q, qr, gates, shared):
        kc, vc, ka, vs, kw, vw = shared
        oc, sb = _cmp_prompt(q, gates, kc, vc, cover_p, e_c, bsz, seq)
        o = _selwin_prompt(qr, sb, gates, oc, ka, vs, kw, vw, e_s, e_w, bsz, seq)
        return o.reshape(bsz * seq, d)

    y_p, a_p, (rows_p, win_p) = run_trunk(x_prompt.reshape(bsz * seq, d), mods_p, mkv_p, mod_spec_p, tabs_p,
                                          tab_spec_p, tm_p, mix_a_p, shared_p, mix_b_p)

    m_s = nb * t_new
    tm_s = min(m_s, 512)
    mods_s = [[jnp.repeat(mod_all[l, bsz:bsz + nb, k], t_new, axis=0) for k in range(6)] for l in range(depth)]
    mkv_s = [jnp.repeat(mkv_all[bsz:bsz + nb, k], t_new, axis=0) for k in range(2)]
    cos_s, sin_s = _rope_tables(past + jnp.arange(t_new))
    tabs_s = (jnp.tile(cos_s, (nb, 1)), jnp.tile(sin_s, (nb, 1)))
    caches = [c.transpose(0, 1, 3, 4, 5, 2).reshape(c.shape[0], nb, 2 * d, c.shape[2])
              for c in (cache_a0, cache_a1, cache_a2)]
    a_state_s = [None] * len(A_PATTERNS)

    def tab_spec_s(tm):
        return pl.BlockSpec((tm, LANES), lambda i, j: (i, 0))

    def mix_a_s(layer, qkv):
        outs, lses = [], []
        for grp in range(len(A_PATTERNS)):
            o, l, a_state_s[grp] = _dil_sample(qkv[grp][0], caches[grp], a_state_s[grp], layer, grp, nb, t_new)
            outs.append(o)
            lses.append(l)
        return outs + lses, None

    n_c_s = (past + t_new - CMP_LEN) // CMP_STRIDE + 1
    n_s_s = -(-(past + t_new) // SLC_LEN)
    cover_s = _cover_matrix(past // CMP_STRIDE, n_c_s, n_s_s)
    fold = _fold_matrix()
    blk1h_s = (jnp.arange(LANES)[:, None] == jnp.arange(past + LANES)[None, :] // SLC_LEN).astype(BF16)
    table_s = page_table.reshape(-1).astype(jnp.int32)
    pool3 = cache_b_pool.transpose(0, 2, 3, 4, 1).reshape(cache_b_pool.shape[0], 4 * gw, PAGE)
    wcache = cache_b_win.reshape(nb, cache_b_win.shape[1], 2 * gw)

    def shared_s(rows, winr):
        kc, vc = _compress(pool3, table_s, nb, npages, cw, pages_t=True)
        w4 = winr.reshape(nb, t_new, 2, N_KV_B, HEAD_DIM)
        win_full = jnp.concatenate([cache_b_win, w4], axis=1)
        n_keep = min(WIN_B, win_full.shape[1])
        state = (rows.reshape(nb, t_new, 4, N_KV_B, HEAD_DIM), win_full[:, win_full.shape[1] - n_keep:])
        return (kc, vc, rows, winr), state

    def mix_b_s(q, qr, gates, shared):
        kc, vc, rows, winr = shared
        return _nsa_sample(q, qr, gates, kc, vc, rows, wcache, winr, pool3, table_s,
                           (fold, fold.T, cover_s, blk1h_s), nb, t_new, npages)

    y_s, a_s, (rows_s, win_s) = run_trunk(x_sample.reshape(m_s, d), mods_s, mkv_s, _row_mod_spec, tabs_s,
                                          tab_spec_s, tm_s, mix_a_s, shared_s, mix_b_s)

    outs = [y_p.reshape(bsz, seq, d), y_s.reshape(nb, t_new, d)]
    for g, cache in enumerate((cache_a0, cache_a1, cache_a2)):
        outs.append(jnp.stack([st[g] for st in a_p]))
        st_t = a_state_s[g].reshape(cache.shape[0], nb, 2, N_HEADS, HEAD_DIM, cache.shape[2])
        outs.append(st_t.transpose(0, 1, 5, 2, 3, 4))
    outs += [rows_p, rows_s, win_p, win_s]
    return tuple(outs)
```
